```python
import math
import jax, jax.numpy as jnp
from jax import lax
import numpy as np

D_MODEL = 1024
BATCH = 16
SEQ = 2048
DEPTH = 2

H_A = 8
D_A = 64
H_I = 4
D_I = 64
TOPK_MAX = 256
H_B = 4
D_B = 64
D_FF = 2816
CONV_W = 3
PLE_DIM = 256
ROPE_THETA = 10000.0
Q_BLOCK = 128
EPS = 1e-6

W_A = H_A * D_A
W_B = H_B * 2 * D_B
N_IN = W_A + D_A + D_A + H_I * D_I + D_I + H_I + 2 * (H_B * 2 * D_B) + W_B + 2 * D_MODEL

kernel_name = 'hybrid_dsa_diffattn_convffn_ple'


def _split_points():
    sizes = [W_A, D_A, D_A, H_I * D_I, D_I, H_I, H_B * 2 * D_B, H_B * 2 * D_B, W_B, D_MODEL, D_MODEL]
    pts, acc = [], 0
    for s in sizes[:-1]:
        acc += s
        pts.append(acc)
    return pts


def rms_norm(x, g):
    xf = x.astype(jnp.float32)
    y = xf * lax.rsqrt(jnp.mean(xf * xf, axis=-1, keepdims=True) + EPS)
    return (y * g.astype(jnp.float32)).astype(x.dtype)


def rope_tables(length, dim):
    inv = 1.0 / (ROPE_THETA ** (jnp.arange(0, dim, 2, dtype=jnp.float32) / dim))
    ang = jnp.arange(length, dtype=jnp.float32)[:, None] * inv[None, :]
    return jnp.cos(ang), jnp.sin(ang)


def apply_rope(x, cos, sin):
    half = x.shape[-1] // 2
    shape = (cos.shape[0],) + (1,) * (x.ndim - 3) + (half,)
    c, s = cos.reshape(shape), sin.reshape(shape)
    xf = x.astype(jnp.float32)
    x1, x2 = xf[..., :half], xf[..., half:]
    return jnp.concatenate([x1 * c - x2 * s, x2 * c + x1 * s], axis=-1).astype(x.dtype)


def gather_rows(t, idx):
    return jax.vmap(lambda tb, ib: tb[ib])(t, idx)


def token_mixers(u, w_in, g_qa, g_ka, g_qb, g_kb, lam_q1, lam_k1, lam_q2, lam_k2,
                 g_subln, w_branch_a, w_branch_b, w_out, lam_init, cos, sin):
    bsz, length, _ = u.shape
    k_top = min(TOPK_MAX, length // 4)
    proj = u @ w_in
    qa, ka, va, qi, ki, wi, qb, kb, vb, gate_a, gate_b = jnp.split(proj, _split_points(), axis=-1)

    qa = apply_rope(rms_norm(qa.reshape(bsz, length, H_A, D_A), g_qa), cos, sin)
    ka = apply_rope(rms_norm(ka, g_ka), cos, sin)
    qi = apply_rope(qi.reshape(bsz, length, H_I, D_I), cos, sin)
    ki = apply_rope(ki, cos, sin)
    wi = wi * (H_I ** -0.5 * D_I ** -0.5)

    qb = apply_rope(rms_norm(qb.reshape(bsz, length, H_B, 2, D_B), g_qb), cos, sin)
    kb = apply_rope(rms_norm(kb.reshape(bsz, length, H_B, 2, D_B), g_kb), cos, sin)
    q1, q2 = qb[..., 0, :], qb[..., 1, :]
    k1, k2 = kb[..., 0, :], kb[..., 1, :]
    vb = vb.reshape(bsz, length, H_B, 2 * D_B)
    lam = (jnp.exp(jnp.sum(lam_q1.astype(jnp.float32) * lam_k1.astype(jnp.float32)))
           - jnp.exp(jnp.sum(lam_q2.astype(jnp.float32) * lam_k2.astype(jnp.float32))) + lam_init)

    scale_a = D_A ** -0.5
    scale_b = D_B ** -0.5
    s_pos = jnp.arange(length)

    def qslice(t, start):
        return lax.dynamic_slice_in_dim(t, start, Q_BLOCK, axis=1)

    def block(start):
        t_pos = start + jnp.arange(Q_BLOCK)
        causal = s_pos[None, :] <= t_pos[:, None]
        rel = jax.nn.relu(jnp.einsum('bthd,bsd->bths', qslice(qi, start), ki).astype(jnp.float32))
        iscore = jnp.einsum('bths,bth->bts', rel, qslice(wi, start).astype(jnp.float32))
        iscore = jnp.where(causal[None], iscore, -jnp.inf)
        _, sel = lax.top_k(iscore, k_top)
        k_sel = gather_rows(ka, sel)
        v_sel = gather_rows(va, sel)
        logit = jnp.einsum('bthd,btkd->bthk', qslice(qa, start), k_sel).astype(jnp.float32) * scale_a
        valid = (sel <= t_pos[None, :, None])[:, :, None, :]
        attn_a = jax.nn.softmax(jnp.where(valid, logit, -jnp.inf), axis=-1)
        o_a = jnp.einsum('bthk,btkd->bthd', attn_a.astype(va.dtype), v_sel)
        s1 = jnp.einsum('bthd,bshd->bhts', qslice(q1, start), k1).astype(jnp.float32) * scale_b
        s2 = jnp.einsum('bthd,bshd->bhts', qslice(q2, start), k2).astype(jnp.float32) * scale_b
        a1 = jax.nn.softmax(jnp.where(causal, s1, -jnp.inf), axis=-1)
        a2 = jax.nn.softmax(jnp.where(causal, s2, -jnp.inf), axis=-1)
        o_b = jnp.einsum('bhts,bshe->bthe', (a1 - lam * a2).astype(vb.dtype), vb)
        return o_a, o_b

    starts = jnp.arange(0, length, Q_BLOCK, dtype=jnp.int32)
    o_a, o_b = lax.map(block, starts)
    o_a = jnp.moveaxis(o_a, 0, 1).reshape(bsz, length, W_A)
    o_b = jnp.moveaxis(o_b, 0, 1).reshape(bsz, length, H_B, 2 * D_B)
    o_b = (rms_norm(o_b, g_subln) * (1.0 - lam_init)).reshape(bsz, length, W_B)

    mix = jax.nn.sigmoid(gate_a) * (o_a @ w_branch_a) + jax.nn.sigmoid(gate_b) * (o_b @ w_branch_b)
    return mix @ w_out


def conv_ffn(u, w_up, conv_w, conv_b, w_down):
    up = u @ w_up
    pad = jnp.pad(up, ((0, 0), (CONV_W - 1, 0), (0, 0)))
    length = up.shape[1]
    conv = conv_b + sum(pad[:, j:j + length] * conv_w[j] for j in range(CONV_W))
    gate, val = jnp.split(conv, 2, axis=-1)
    return (jax.nn.gelu(gate, approximate=True) * val) @ w_down


def setup_inputs(seed: int = 0) -> dict:
    key = jax.random.key(seed)
    ks = jax.random.split(key, 24)
    f32 = jnp.float32

    def nrm(k, shape, scale):
        return jax.random.normal(k, shape, f32) * scale

    def gain(k, shape):
        return 1.0 + 0.05 * jax.random.normal(k, shape, f32)

    return {
        'x': nrm(ks[0], (BATCH, SEQ, D_MODEL), 1.0),
        'p': nrm(ks[1], (DEPTH, BATCH, SEQ, PLE_DIM), 1.0),
        'g_mix_norm': gain(ks[2], (DEPTH, D_MODEL)),
        'w_in': nrm(ks[3], (DEPTH, D_MODEL, N_IN), D_MODEL ** -0.5),
        'g_qa': gain(ks[4], (DEPTH, D_A)),
        'g_ka': gain(ks[5], (DEPTH, D_A)),
        'g_qb': gain(ks[6], (DEPTH, D_B)),
        'g_kb': gain(ks[7], (DEPTH, D_B)),
        'lam_q1': nrm(ks[8], (DEPTH, D_B), 0.1),
        'lam_k1': nrm(ks[9], (DEPTH, D_B), 0.1),
        'lam_q2': nrm(ks[10], (DEPTH, D_B), 0.1),
        'lam_k2': nrm(ks[11], (DEPTH, D_B), 0.1),
        'g_subln': gain(ks[12], (DEPTH, 2 * D_B)),
        'w_branch_a': nrm(ks[13], (DEPTH, W_A, D_MODEL), W_A ** -0.5),
        'w_branch_b': nrm(ks[14], (DEPTH, W_B, D_MODEL), W_B ** -0.5),
        'w_out': nrm(ks[15], (DEPTH, D_MODEL, D_MODEL), D_MODEL ** -0.5),
        'g_ffn_norm': gain(ks[16], (DEPTH, D_MODEL)),
        'w_up': nrm(ks[17], (DEPTH, D_MODEL, 2 * D_FF), D_MODEL ** -0.5),
        'conv_w': nrm(ks[18], (DEPTH, CONV_W, 2 * D_FF), CONV_W ** -0.5),
        'conv_b': nrm(ks[19], (DEPTH, 2 * D_FF), 0.02),
        'w_down': nrm(ks[20], (DEPTH, D_FF, D_MODEL), D_FF ** -0.5),
        'g_ple_norm': gain(ks[21], (DEPTH, D_MODEL)),
        'w_ple_gate': nrm(ks[22], (DEPTH, D_MODEL, D_MODEL), D_MODEL ** -0.5),
        'w_ple_proj': nrm(ks[23], (DEPTH, PLE_DIM, D_MODEL), PLE_DIM ** -0.5),
    }


def reference(x, p, g_mix_norm, w_in, g_qa, g_ka, g_qb, g_kb, lam_q1, lam_k1, lam_q2, lam_k2,
              g_subln, w_branch_a, w_branch_b, w_out, g_ffn_norm, w_up, conv_w, conv_b, w_down,
              g_ple_norm, w_ple_gate, w_ple_proj):
    length = x.shape[1]
    cos, sin = rope_tables(length, D_A)
    h = x
    for i in range(DEPTH):
        lam_init = 0.8 - 0.6 * math.exp(-0.3 * i)
        u = rms_norm(h, g_mix_norm[i])
        h = h + token_mixers(u, w_in[i], g_qa[i], g_ka[i], g_qb[i], g_kb[i], lam_q1[i], lam_k1[i],
                             lam_q2[i], lam_k2[i], g_subln[i], w_branch_a[i], w_branch_b[i], w_out[i],
                             lam_init, cos, sin)
        h = h + conv_ffn(rms_norm(h, g_ffn_norm[i]), w_up[i], conv_w[i], conv_b[i], w_down[i])
        ple_gate = jax.nn.sigmoid(rms_norm(h, g_ple_norm[i]) @ w_ple_gate[i])
        h = h + ple_gate * (p[i].astype(h.dtype) @ w_ple_proj[i])
    return h
```

```python
import functools
import math

import jax
import jax.numpy as jnp
from jax import lax
from jax.experimental import pallas as pl
from jax.experimental.pallas import tpu as pltpu

H_A, D_A = 8, 64
H_I, D_I = 4, 64
H_B, D_B = 4, 64
TOPK_MAX = 256
CONV_W = 3
ROPE_THETA = 10000.0
EPS = 1e-6

LANES = 128
SUBLANES = 8
VMEM_LIMIT = 56 * 1024 * 1024

F32 = jnp.float32
BF16 = jnp.bfloat16
NEG_MASKED = -2e30
NEG_INIT = -1e30
INT_MIN = -(2 ** 31)


def _nt_dot(a, b):
    return lax.dot_general(a, b, (((1,), (1,)), ((), ())), preferred_element_type=F32)


def _dot(a, b):
    return jnp.dot(a, b, preferred_element_type=F32)


def _cparams(n_axes):
    return pltpu.CompilerParams(dimension_semantics=("arbitrary",) * n_axes,
                                vmem_limit_bytes=VMEM_LIMIT)


def _rms_rows(x, g):
    ms = jnp.mean(x * x, axis=-1, keepdims=True)
    return x * lax.rsqrt(ms + EPS) * g


def _rope(x, cos_t, sin_t):
    w = x.shape[-1]
    reps = w // LANES
    c = jnp.tile(cos_t, (1, reps)) if reps > 1 else cos_t
    s = jnp.tile(sin_t, (1, reps)) if reps > 1 else sin_t
    up = pltpu.roll(x, w - 32, 1)
    dn = pltpu.roll(x, 32, 1)
    lane = lax.broadcasted_iota(jnp.int32, x.shape, 1)
    partner = jnp.where((lane & 63) < 32, up, dn)
    return x * c + partner * s


def _head_norm(x, gsum, g):
    ss = _dot((x * x).astype(BF16), gsum) * (1.0 / 64.0)
    return x * lax.rsqrt(ss + EPS) * g


def _in_proj_kernel(h_ref, g_ref, wqa_ref, wsm_ref, wqi_ref, wwi_ref, wqb_ref, wkb_ref, wvb_ref,
                    cos_ref, sin_ref, gsum_ref, gqa_ref, gka_ref, gqb_ref, gkb_ref,
                    qa_ref, ka_ref, va_ref, qi_ref, ki_ref, wi_ref, qb_ref, kb_ref, vb_ref):
    u = _rms_rows(h_ref[...], g_ref[...]).astype(BF16)
    cos_t, sin_t = cos_ref[...], sin_ref[...]
    gsum = gsum_ref[...]

    qa = _rope(_head_norm(_dot(u, wqa_ref[...]), gsum, gqa_ref[...]), cos_t, sin_t)
    qa_ref[...] = (qa * (D_A ** -0.5)).astype(BF16)

    sm = _dot(u, wsm_ref[...])
    ka = _rope(_head_norm(sm[:, 0:LANES], gsum[0:LANES, 0:LANES], gka_ref[...]), cos_t, sin_t)
    ka_ref[...] = ka.astype(BF16)
    lane = lax.broadcasted_iota(jnp.int32, (sm.shape[0], LANES), 1)
    va_ref[...] = jnp.where(lane < D_A, sm[:, LANES:2 * LANES], 1.0).astype(BF16)
    ki_ref[...] = _rope(sm[:, 2 * LANES:3 * LANES], cos_t, sin_t).astype(BF16)

    qi_ref[...] = _rope(_dot(u, wqi_ref[...]), cos_t, sin_t).astype(BF16)
    wi_ref[...] = _nt_dot(wwi_ref[...], u) * (H_I ** -0.5 * D_I ** -0.5)

    qb = _rope(_head_norm(_dot(u, wqb_ref[...]), gsum, gqb_ref[...]), cos_t, sin_t)
    qb_ref[...] = (qb * (D_B ** -0.5)).astype(BF16)
    kb = _rope(_head_norm(_dot(u, wkb_ref[...]), gsum, gkb_ref[...]), cos_t, sin_t)
    kb_ref[...] = kb.astype(BF16)
    vb_ref[...] = _dot(u, wvb_ref[...]).astype(BF16)


def _in_proj(h, g, w, tabs, seq_len, tm):
    t, d = h.shape
    nseq = seq_len // tm

    def full(a):
        return pl.BlockSpec(a.shape, lambda i: (0,) * a.ndim)

    def rows(width):
        return pl.BlockSpec((tm, width), lambda i: (i, 0))

    tab = pl.BlockSpec((tm, LANES), lambda i: (i % nseq, 0))
    ins = [h, g, w["qa"], w["sm"], w["qi"], w["wi"], w["qb"], w["kb"], w["vb"],
           tabs["cos"], tabs["sin"], tabs["gsum"], w["gqa"], w["gka"], w["gqb"], w["gkb"]]
    in_specs = [rows(d), full(g)] + [full(a) for a in ins[2:9]] + [tab, tab] + [full(a) for a in ins[11:]]
    widths = [H_A * D_A, LANES, LANES, H_I * D_I, LANES, None, 2 * H_B * D_B, 2 * H_B * D_B, 2 * H_B * D_B]
    out_shape, out_specs = [], []
    for wd in widths:
        if wd is None:
            out_shape.append(jax.ShapeDtypeStruct((SUBLANES, t), F32))
            out_specs.append(pl.BlockSpec((SUBLANES, tm), lambda i: (0, i)))
        else:
            out_shape.append(jax.ShapeDtypeStruct((t, wd), BF16))
            out_specs.append(rows(wd))
    return pl.pallas_call(
        _in_proj_kernel, grid=(t // tm,), in_specs=in_specs, out_specs=out_specs, out_shape=out_shape,
        compiler_params=_cparams(1), name="in_proj")(*ins)


def _sortable_key(x):
    i = lax.bitcast_convert_type(x + 0.0, jnp.int32)
    return i ^ ((i >> 31) & 0x7FFFFFFF)


def _dsa_kernel(qa_ref, qi_ref, wi_ref, ki_ref, ka_ref, va_ref, o_ref,
                key_ref, bias_ref, qs_ref, s_ref, p_ref, m_ref, acc_ref, xsel_ref,
                *, tq, k_top, count_rows):
    j = pl.program_id(1)
    nkb = j + 1
    tk = tq
    nh = H_A
    groups = tk // SUBLANES

    lane_q = lax.broadcasted_iota(jnp.int32, (tq, LANES), 1)
    lo_half = lane_q < 64

    qi = qi_ref[...]
    qi_heads = []
    for h in range(H_I):
        pair = qi[:, (h // 2) * LANES:(h // 2 + 1) * LANES]
        keep = lo_half if h % 2 == 0 else jnp.logical_not(lo_half)
        qi_heads.append(jnp.where(keep, pair, jnp.zeros_like(pair)))
    wi = wi_ref[...]
    t_idx = j * tq + lax.broadcasted_iota(jnp.int32, (tk, tq), 1)
    s_loc = lax.broadcasted_iota(jnp.int32, (tk, tq), 0)

    def score_body(kb, carry):
        kblk = ki_ref[pl.ds(pl.multiple_of(kb * tk, tk), tk), :]
        acc = jnp.zeros((tk, tq), F32)
        for h in range(H_I):
            acc = acc + wi[h:h + 1, :] * jnp.maximum(_nt_dot(kblk, qi_heads[h]), 0.0)
        key = jnp.where(kb * tk + s_loc <= t_idx, _sortable_key(acc), INT_MIN)
        key_ref[kb] = key.reshape(groups, SUBLANES, tq)
        return carry

    lax.fori_loop(0, nkb, score_body, 0)

    ngroups_chunk = count_rows // SUBLANES
    nchunks = nkb * (tk // count_rows)

    def count(pred):
        def body(c, acc):
            kb = c // (tk // count_rows)
            g0 = (c % (tk // count_rows)) * ngroups_chunk
            blk = key_ref[kb, pl.ds(pl.multiple_of(g0, ngroups_chunk), ngroups_chunk)]
            return acc + jnp.sum(pred(blk, c * count_rows).astype(jnp.int32), axis=0)
        acc = lax.fori_loop(0, nchunks, body, jnp.zeros((SUBLANES, tq), jnp.int32))
        return jnp.sum(acc, axis=0, keepdims=True)

    def bcast8(v):
        return jnp.broadcast_to(v, (SUBLANES, tq))[None]

    def thr_body(step, thr):
        bit = jnp.where(step == 0, 0, jnp.left_shift(1, jnp.maximum(31 - step, 0)))
        cand = jnp.where(step == 0, jnp.zeros_like(thr), thr | bit)
        c8 = bcast8(cand)
        n_ge = count(lambda blk, r0: blk >= c8)
        return jnp.where(n_ge >= k_top, cand, thr)

    thr = lax.fori_loop(0, 32, thr_body, jnp.full((1, tq), INT_MIN, jnp.int32))
    thr8 = bcast8(thr)
    n_gt = count(lambda blk, r0: blk > thr8)
    n_eq = count(lambda blk, r0: blk == thr8)
    need = k_top - n_gt

    xsel_ref[...] = jnp.full((SUBLANES, tq), nkb * tk, jnp.int32)
    tie_overflow = jnp.max(jnp.where((n_eq > need) & (thr != INT_MIN), 1, 0)) > 0

    @pl.when(tie_overflow)
    def _():
        row_in_chunk = (lax.broadcasted_iota(jnp.int32, (ngroups_chunk, SUBLANES, tq), 0) * SUBLANES
                        + lax.broadcasted_iota(jnp.int32, (ngroups_chunk, SUBLANES, tq), 1))
        nbits = max(1, (key_ref.shape[0] * tk - 1).bit_length())

        def idx_body(step, x):
            cand = x | jnp.left_shift(1, nbits - 1 - step)
            c8 = bcast8(cand)
            n_before = count(lambda blk, r0: (blk == thr8) & (row_in_chunk + r0 < c8))
            return jnp.where(n_before < need, cand, x)

        x = lax.fori_loop(0, nbits, idx_body, jnp.zeros((1, tq), jnp.int32))
        xsel_ref[...] = jnp.broadcast_to(x + 1, (SUBLANES, tq))

    xsel = xsel_ref[0:1, :]

    def bias_body(kb, carry):
        key = key_ref[kb].reshape(tk, tq)
        s_idx = kb * tk + s_loc
        sel = (key > thr) | ((key == thr) & (s_idx < xsel))
        sel = sel & (s_idx <= t_idx)
        bias_ref[kb] = jnp.where(sel, 0.0, NEG_MASKED).astype(F32).T
        return carry

    lax.fori_loop(0, nkb, bias_body, 0)

    qa = qa_ref[...]
    for h in range(nh):
        pair = qa[:, (h // 2) * LANES:(h // 2 + 1) * LANES]
        keep = lo_half if h % 2 == 0 else jnp.logical_not(lo_half)
        qs_ref[h * tq:(h + 1) * tq, :] = jnp.where(keep, pair, jnp.zeros_like(pair))
    m_ref[...] = jnp.full(m_ref.shape, NEG_INIT, F32)
    acc_ref[...] = jnp.zeros(acc_ref.shape, F32)

    def attn_body(kb, carry):
        k0 = pl.multiple_of(kb * tk, tk)
        s_ref[...] = _nt_dot(qs_ref[...], ka_ref[pl.ds(k0, tk), :])
        bias = bias_ref[kb]

        def head_body(h, c2):
            rows = pl.ds(pl.multiple_of(h * tq, tq), tq)
            s = s_ref[rows, :] + bias
            m_prev = m_ref[rows, :]
            m_new = jnp.maximum(m_prev, jnp.max(s, axis=1, keepdims=True))
            p = jnp.exp(s - jnp.tile(m_new, (1, tk // LANES)))
            p_ref[rows, :] = p.astype(BF16)
            acc_ref[rows, :] = acc_ref[rows, :] * jnp.exp(m_prev - m_new)
            m_ref[rows, :] = m_new
            return c2

        lax.fori_loop(0, nh, head_body, 0)
        acc_ref[...] += _dot(p_ref[...], va_ref[pl.ds(k0, tk), :])
        return carry

    lax.fori_loop(0, nkb, attn_body, 0)

    for pr in range(nh // 2):
        outs = []
        for h in (2 * pr, 2 * pr + 1):
            acc = acc_ref[h * tq:(h + 1) * tq, :]
            outs.append(acc / pltpu.roll(acc, 64, 1))
        merged = jnp.where(lo_half, outs[0], pltpu.roll(outs[1], 64, 1))
        o_ref[:, pr * LANES:(pr + 1) * LANES] = merged.astype(BF16)


def _dsa(qa, qi, wi_t, ki, ka, va, tq):
    b, l, _ = qa.shape
    nq = l // tq
    k_top = min(TOPK_MAX, l // 4)
    kern = functools.partial(_dsa_kernel, tq=tq, k_top=k_top, count_rows=128)
    per_q = lambda w: pl.BlockSpec((None, tq, w), lambda bi, j: (bi, j, 0))
    per_b = pl.BlockSpec((None, l, LANES), lambda bi, j: (bi, 0, 0))
    return pl.pallas_call(
        kern, grid=(b, nq),
        in_specs=[per_q(H_A * D_A), per_q(H_I * D_I),
                  pl.BlockSpec((SUBLANES, tq), lambda bi, j: (0, bi * nq + j)),
                  per_b, per_b, per_b],
        out_specs=per_q(H_A * D_A),
        out_shape=jax.ShapeDtypeStruct((b, l, H_A * D_A), BF16),
        scratch_shapes=[
            pltpu.VMEM((nq, tq // SUBLANES, SUBLANES, tq), jnp.int32),
            pltpu.VMEM((nq, tq, tq), F32),
            pltpu.VMEM((H_A * tq, LANES), BF16),
            pltpu.VMEM((H_A * tq, tq), F32),
            pltpu.VMEM((H_A * tq, tq), BF16),
            pltpu.VMEM((H_A * tq, LANES), F32),
            pltpu.VMEM((H_A * tq, LANES), F32),
            pltpu.VMEM((SUBLANES, tq), jnp.int32),
        ],
        compiler_params=_cparams(2), name="dsa")(qa, qi, wi_t, ki, ka, va)


def _diff_kernel(qb_ref, kb_ref, vb_ref, lam_ref, gsub_ref, o_ref, qs_ref, m_ref, l_ref, acc_ref,
                 *, tq, lam_init):
    j = pl.program_id(1)
    tk = tq
    lamv = lam_ref[...]
    lam = (jnp.exp(jnp.sum(lamv[0:1] * lamv[1:2], axis=1, keepdims=True))
           - jnp.exp(jnp.sum(lamv[2:3] * lamv[3:4], axis=1, keepdims=True)) + lam_init)
    lane_q = lax.broadcasted_iota(jnp.int32, (tq, LANES), 1)
    lo_half = lane_q < 64
    row = lax.broadcasted_iota(jnp.int32, (2 * tq, tk), 0)
    t_loc = jnp.where(row >= tq, row - tq, row)
    s_loc = lax.broadcasted_iota(jnp.int32, (2 * tq, tk), 1)
    causal_diag = s_loc <= t_loc

    for h in range(H_B):
        cols = slice(h * LANES, (h + 1) * LANES)
        pair = qb_ref[:, cols]
        qs_ref[0:tq, :] = jnp.where(lo_half, pair, jnp.zeros_like(pair))
        qs_ref[tq:2 * tq, :] = jnp.where(lo_half, jnp.zeros_like(pair), pair)
        m_ref[...] = jnp.full(m_ref.shape, NEG_INIT, F32)
        l_ref[...] = jnp.zeros(l_ref.shape, F32)
        acc_ref[...] = jnp.zeros(acc_ref.shape, F32)

        def step(kb, masked, cols=cols):
            k0 = pl.multiple_of(kb * tk, tk)
            s = _nt_dot(qs_ref[...], kb_ref[pl.ds(k0, tk), cols])
            if masked:
                s = jnp.where(causal_diag, s, NEG_MASKED)
            m_prev = m_ref[...]
            m_new = jnp.maximum(m_prev, jnp.max(s, axis=1, keepdims=True))
            alpha = jnp.exp(m_prev - m_new)
            p = jnp.exp(s - jnp.tile(m_new, (1, tk // LANES)))
            l_ref[...] = alpha * l_ref[...] + jnp.sum(p, axis=1, keepdims=True)
            acc_ref[...] = alpha * acc_ref[...] + _dot(p.astype(BF16), vb_ref[pl.ds(k0, tk), cols])
            m_ref[...] = m_new

        def body(kb, carry):
            step(kb, False)
            return carry

        lax.fori_loop(0, j, body, 0)
        step(j, True)

        o = acc_ref[...] / l_ref[...]
        o = o[0:tq] - lam * o[tq:2 * tq]
        y = _rms_rows(o, gsub_ref[...]) * (1.0 - lam_init)
        o_ref[:, cols] = y.astype(BF16)


def _diffattn(qb, kb, vb, lam_p, gsub, tq, lam_init):
    b, l, w = qb.shape
    nq = l // tq
    kern = functools.partial(_diff_kernel, tq=tq, lam_init=lam_init)
    per_q = pl.BlockSpec((None, tq, w), lambda bi, j: (bi, j, 0))
    per_b = pl.BlockSpec((None, l, w), lambda bi, j: (bi, 0, 0))
    full = lambda a: pl.BlockSpec(a.shape, lambda bi, j: (0,) * a.ndim)
    return pl.pallas_call(
        kern, grid=(b, nq),
        in_specs=[per_q, per_b, per_b, full(lam_p), full(gsub)],
        out_specs=per_q,
        out_shape=jax.ShapeDtypeStruct((b, l, w), BF16),
        scratch_shapes=[pltpu.VMEM((2 * tq, LANES), BF16),
                        pltpu.VMEM((2 * tq, LANES), F32),
                        pltpu.VMEM((2 * tq, LANES), F32),
                        pltpu.VMEM((2 * tq, LANES), F32)],
        compiler_params=_cparams(2), name="diffattn")(qb, kb, vb, lam_p, gsub)


def _merge_kernel(h_ref, g_ref, oa_ref, ob_ref, wg_ref, wa_ref, wb_ref, wo_ref, out_ref):
    h = h_ref[...]
    d = h.shape[-1]
    u = _rms_rows(h, g_ref[...]).astype(BF16)
    gates = jax.nn.sigmoid(_dot(u, wg_ref[...]))
    mix = (gates[:, 0:d] * _dot(oa_ref[...], wa_ref[...])
           + gates[:, d:2 * d] * _dot(ob_ref[...], wb_ref[...]))
    out_ref[...] = h + _dot(mix.astype(BF16), wo_ref[...])


def _merge(h, g, oa, ob, wg, wa, wb, wo, tm):
    t, d = h.shape
    rows = lambda w: pl.BlockSpec((tm, w), lambda i: (i, 0))
    full = lambda a: pl.BlockSpec(a.shape, lambda i: (0,) * a.ndim)
    return pl.pallas_call(
        _merge_kernel, grid=(t // tm,),
        in_specs=[rows(d), full(g), rows(oa.shape[1]), rows(ob.shape[1]), full(wg), full(wa), full(wb), full(wo)],
        out_specs=rows(d), out_shape=jax.ShapeDtypeStruct((t, d), F32),
        compiler_params=_cparams(1), name="merge")(h, g, oa, ob, wg, wa, wb, wo)


def _gelu_tanh(x):
    return 0.5 * x * (1.0 + jnp.tanh(math.sqrt(2.0 / math.pi) * (x + 0.044715 * (x * x * x))))


def _ffn_kernel(h_ref, g_ref, wug_ref, wuv_ref, cg_ref, cv_ref, wd_ref, out_ref, tail_g_ref, tail_v_ref, acc_ref,
                *, tiles_per_seq):
    i = pl.program_id(0)
    tm = h_ref.shape[0]
    h = h_ref[...]
    u = _rms_rows(h, g_ref[...]).astype(BF16)
    seq_start = (i % tiles_per_seq) == 0
    row = lax.broadcasted_iota(jnp.int32, (tm, 1), 0)
    acc_ref[...] = jnp.zeros(acc_ref.shape, F32)

    def causal_conv(x, tail_ref, c, cw):
        prev = jnp.where(seq_start, 0.0, tail_ref[c])
        tail_ref[c] = x[tm - SUBLANES:tm, :]
        x1 = pltpu.roll(x, 1, 0)
        x2 = pltpu.roll(x, 2, 0)
        x1 = jnp.where(row == 0, prev[7:8, :], x1)
        x2 = jnp.where(row == 0, prev[6:7, :], jnp.where(row == 1, prev[7:8, :], x2))
        return cw[3:4, :] + x2 * cw[0:1, :] + x1 * cw[1:2, :] + x * cw[2:3, :]

    def chunk(c, carry):
        gate = causal_conv(_dot(u, wug_ref[c]), tail_g_ref, c, cg_ref[c])
        val = causal_conv(_dot(u, wuv_ref[c]), tail_v_ref, c, cv_ref[c])
        act = (_gelu_tanh(gate) * val).astype(BF16)
        acc_ref[...] += _dot(act, wd_ref[c])
        return carry

    lax.fori_loop(0, wug_ref.shape[0], chunk, 0)
    out_ref[...] = h + acc_ref[...]


def _convffn(h, g, wug, wuv, cg, cv, wd, seq_len, tm):
    t, d = h.shape
    nc, _, cw = wug.shape
    rows = pl.BlockSpec((tm, d), lambda i: (i, 0))
    full = lambda a: pl.BlockSpec(a.shape, lambda i: (0,) * a.ndim)
    kern = functools.partial(_ffn_kernel, tiles_per_seq=seq_len // tm)
    return pl.pallas_call(
        kern, grid=(t // tm,),
        in_specs=[rows, full(g), full(wug), full(wuv), full(cg), full(cv), full(wd)],
        out_specs=rows, out_shape=jax.ShapeDtypeStruct((t, d), F32),
        scratch_shapes=[pltpu.VMEM((nc, SUBLANES, cw), F32), pltpu.VMEM((nc, SUBLANES, cw), F32),
                        pltpu.VMEM((tm, d), F32)],
        compiler_params=_cparams(1), name="convffn")(h, g, wug, wuv, cg, cv, wd)


def _ple_kernel(h_ref, g_ref, p_ref, wg_ref, wp_ref, out_ref):
    h = h_ref[...]
    u = _rms_rows(h, g_ref[...]).astype(BF16)
    gate = jax.nn.sigmoid(_dot(u, wg_ref[...]))
    out_ref[...] = h + gate * _dot(p_ref[...].astype(BF16), wp_ref[...])


def _ple(h, g, p, wg, wp, tm):
    t, d = h.shape
    rows = lambda w: pl.BlockSpec((tm, w), lambda i: (i, 0))
    full = lambda a: pl.BlockSpec(a.shape, lambda i: (0,) * a.ndim)
    return pl.pallas_call(
        _ple_kernel, grid=(t // tm,),
        in_specs=[rows(d), full(g), rows(p.shape[1]), full(wg), full(wp)],
        out_specs=rows(d), out_shape=jax.ShapeDtypeStruct((t, d), F32),
        compiler_params=_cparams(1), name="ple")(h, g, p, wg, wp)


def _rope_tables(length):
    inv = 1.0 / (ROPE_THETA ** (jnp.arange(0, D_A, 2, dtype=F32) / D_A))
    ang = jnp.arange(length, dtype=F32)[:, None] * inv[None, :]
    c, s = jnp.cos(ang), jnp.sin(ang)
    return jnp.tile(c, (1, 4)), jnp.tile(jnp.concatenate([-s, s], axis=1), (1, 2))


def _pick_tile(n, pref):
    tile = min(n, pref)
    assert n % tile == 0, (n, tile)
    return tile


def kernel(x, p, g_mix_norm, w_in, g_qa, g_ka, g_qb, g_kb, lam_q1, lam_k1, lam_q2, lam_k2, g_subln,
           w_branch_a, w_branch_b, w_out, g_ffn_norm, w_up, conv_w, conv_b, w_down, g_ple_norm,
           w_ple_gate, w_ple_proj):
    bsz, seq_len, d_model = x.shape
    depth = w_in.shape[0]
    d_ff = w_down.shape[1]
    t = bsz * seq_len
    tq = _pick_tile(seq_len, 256)
    tm = _pick_tile(seq_len, 512)
    cw = 256
    assert d_ff % cw == 0
    nc = d_ff // cw

    cos_t, sin_t = _rope_tables(seq_len)
    gsum = jnp.kron(jnp.eye(8, dtype=F32), jnp.ones((64, 64), F32)).astype(BF16)
    tabs = {"cos": cos_t, "sin": sin_t, "gsum": gsum}

    sizes = [H_A * D_A, D_A, D_A, H_I * D_I, D_I, H_I, 2 * H_B * D_B, 2 * H_B * D_B, 2 * H_B * D_B, d_model, d_model]
    offs = [0]
    for sz in sizes:
        offs.append(offs[-1] + sz)
    assert offs[-1] == w_in.shape[2]

    h = x.reshape(t, d_model)
    for i in range(depth):
        lam_init = 0.8 - 0.6 * math.exp(-0.3 * i)
        wi_full = w_in[i]
        seg = lambda k: wi_full[:, offs[k]:offs[k + 1]]
        wts = {
            "qa": seg(0).astype(BF16),
            "sm": jnp.concatenate([seg(1), seg(1), seg(2), seg(2), seg(4), seg(4)], axis=1).astype(BF16),
            "qi": seg(3).astype(BF16),
            "wi": jnp.pad(seg(5).T, ((0, SUBLANES - H_I), (0, 0))).astype(BF16),
            "qb": seg(6).astype(BF16), "kb": seg(7).astype(BF16), "vb": seg(8).astype(BF16),
            "gqa": jnp.tile(g_qa[i], H_A)[None, :], "gka": jnp.tile(g_ka[i], 2)[None, :],
            "gqb": jnp.tile(g_qb[i], 2 * H_B)[None, :], "gkb": jnp.tile(g_kb[i], 2 * H_B)[None, :],
        }
        qa, ka, va, qi, ki, wi_t, qb, kb, vb = _in_proj(h, g_mix_norm[i][None, :], wts, tabs, seq_len, tm)
        r3 = lambda a: a.reshape(bsz, seq_len, a.shape[-1])
        o_a = _dsa(r3(qa), r3(qi), wi_t, r3(ki), r3(ka), r3(va), tq)
        lam_p = jnp.stack([lam_q1[i], lam_k1[i], lam_q2[i], lam_k2[i]])
        o_b = _diffattn(r3(qb), r3(kb), r3(vb), lam_p, g_subln[i][None, :], tq, lam_init)

        w_gate = jnp.concatenate([seg(9), seg(10)], axis=1).astype(BF16)
        h = _merge(h, g_mix_norm[i][None, :], o_a.reshape(t, -1), o_b.reshape(t, -1), w_gate,
                   w_branch_a[i].astype(BF16), w_branch_b[i].astype(BF16), w_out[i].astype(BF16), tm)

        def chunks(a):
            return a.reshape(a.shape[0], nc, cw).transpose(1, 0, 2)
        wug = chunks(w_up[i][:, :d_ff]).astype(BF16)
        wuv = chunks(w_up[i][:, d_ff:]).astype(BF16)
        conv = jnp.concatenate([conv_w[i], conv_b[i][None, :]], axis=0)
        conv = jnp.pad(conv, ((0, SUBLANES - CONV_W - 1), (0, 0)))
        cg, cv = chunks(conv[:, :d_ff]), chunks(conv[:, d_ff:])
        wd = w_down[i].reshape(nc, cw, d_model).astype(BF16)
        h = _convffn(h, g_ffn_norm[i][None, :], wug, wuv, cg, cv, wd, seq_len, tm)

        h = _ple(h, g_ple_norm[i][None, :], p[i].reshape(t, -1), w_ple_gate[i].astype(BF16),
                 w_ple_proj[i].astype(BF16), tm)
    return h.reshape(bsz, seq_len, d_model)
```

```python
import functools
import math

import jax
import jax.numpy as jnp
from jax import lax
from jax.experimental import pallas as pl
from jax.experimental.pallas import tpu as pltpu

H_A, D_A = 8, 64
H_I, D_I = 4, 64
H_B, D_B = 4, 64
TOPK_MAX = 256
CONV_W = 3
ROPE_THETA = 10000.0
EPS = 1e-6

LANES = 128
SUBLANES = 8
VMEM_LIMIT = 56 * 1024 * 1024

F32 = jnp.float32
BF16 = jnp.bfloat16
NEG_MASKED = -2e30
NEG_INIT = -1e30
INT_MIN = -(2 ** 31)
LOG2E = math.log2(math.e)


def _nt_dot(a, b):
    return lax.dot_general(a, b, (((1,), (1,)), ((), ())), preferred_element_type=F32)


def _dot(a, b):
    return jnp.dot(a, b, preferred_element_type=F32)


def _cparams(n_axes):
    return pltpu.CompilerParams(dimension_semantics=("arbitrary",) * n_axes,
                                vmem_limit_bytes=VMEM_LIMIT)


def _rms_rows(x, g):
    ms = jnp.mean(x * x, axis=-1, keepdims=True)
    return x * lax.rsqrt(ms + EPS) * g


def _rope(x, cos_t, sin_t):
    w = x.shape[-1]
    reps = w // LANES
    c = jnp.tile(cos_t, (1, reps)) if reps > 1 else cos_t
    s = jnp.tile(sin_t, (1, reps)) if reps > 1 else sin_t
    up = pltpu.roll(x, w - 32, 1)
    dn = pltpu.roll(x, 32, 1)
    lane = lax.broadcasted_iota(jnp.int32, x.shape, 1)
    partner = jnp.where((lane & 63) < 32, up, dn)
    return x * c + partner * s


def _head_norm(x, gsum, g):
    ss = _dot((x * x).astype(BF16), gsum) * (1.0 / 64.0)
    return x * lax.rsqrt(ss + EPS) * g


def _in_proj_kernel(h_ref, g_ref, wqa_ref, wsm_ref, wqi_ref, wwi_ref, wqb_ref, wkb_ref, wvb_ref,
                    cos_ref, sin_ref, gsum_ref, gqa_ref, gka_ref, gqb_ref, gkb_ref,
                    qa_ref, ka_ref, va_ref, qi_ref, ki_ref, wi_ref, qb_ref, kb_ref, vb_ref):
    u = _rms_rows(h_ref[...], g_ref[...]).astype(BF16)
    cos_t, sin_t = cos_ref[...], sin_ref[...]
    gsum = gsum_ref[...]

    qa = _rope(_head_norm(_dot(u, wqa_ref[...]), gsum, gqa_ref[...]), cos_t, sin_t)
    qa_ref[...] = (qa * (D_A ** -0.5 * LOG2E)).astype(BF16)

    sm = _dot(u, wsm_ref[...])
    ka = _rope(_head_norm(sm[:, 0:LANES], gsum[0:LANES, 0:LANES], gka_ref[...]), cos_t, sin_t)
    ka_ref[...] = ka.astype(BF16)
    lane = lax.broadcasted_iota(jnp.int32, (sm.shape[0], LANES), 1)
    va_ref[...] = jnp.where(lane < D_A, sm[:, LANES:2 * LANES], 1.0).astype(BF16)
    ki_ref[...] = _rope(sm[:, 2 * LANES:3 * LANES], cos_t, sin_t).astype(BF16)

    qi_ref[...] = _rope(_dot(u, wqi_ref[...]), cos_t, sin_t).astype(BF16)
    wi_ref[...] = _nt_dot(wwi_ref[...], u) * (H_I ** -0.5 * D_I ** -0.5)

    qb = _rope(_head_norm(_dot(u, wqb_ref[...]), gsum, gqb_ref[...]), cos_t, sin_t)
    qb_ref[...] = (qb * (D_B ** -0.5 * LOG2E)).astype(BF16)
    kb = _rope(_head_norm(_dot(u, wkb_ref[...]), gsum, gkb_ref[...]), cos_t, sin_t)
    kb_ref[...] = kb.astype(BF16)
    vb_ref[...] = _dot(u, wvb_ref[...]).astype(BF16)


def _in_proj(h, g, w, tabs, seq_len, tm):
    t, d = h.shape
    nseq = seq_len // tm

    def full(a):
        return pl.BlockSpec(a.shape, lambda i: (0,) * a.ndim)

    def rows(width):
        return pl.BlockSpec((tm, width), lambda i: (i, 0))

    tab = pl.BlockSpec((tm, LANES), lambda i: (i % nseq, 0))
    ins = [h, g, w["qa"], w["sm"], w["qi"], w["wi"], w["qb"], w["kb"], w["vb"],
           tabs["cos"], tabs["sin"], tabs["gsum"], w["gqa"], w["gka"], w["gqb"], w["gkb"]]
    in_specs = [rows(d), full(g)] + [full(a) for a in ins[2:9]] + [tab, tab] + [full(a) for a in ins[11:]]
    widths = [H_A * D_A, LANES, LANES, H_I * D_I, LANES, None, 2 * H_B * D_B, 2 * H_B * D_B, 2 * H_B * D_B]
    out_shape, out_specs = [], []
    for wd in widths:
        if wd is None:
            out_shape.append(jax.ShapeDtypeStruct((SUBLANES, t), F32))
            out_specs.append(pl.BlockSpec((SUBLANES, tm), lambda i: (0, i)))
        else:
            out_shape.append(jax.ShapeDtypeStruct((t, wd), BF16))
            out_specs.append(rows(wd))
    return pl.pallas_call(
        _in_proj_kernel, grid=(t // tm,), in_specs=in_specs, out_specs=out_specs, out_shape=out_shape,
        compiler_params=_cparams(1), name="in_proj")(*ins)


def _sortable_key(x):
    i = lax.bitcast_convert_type(x + 0.0, jnp.int32)
    return i ^ ((i >> 31) & 0x7FFFFFFF)


def _dsa_kernel(qa_ref, qi_ref, wi_ref, ki_ref, ka_ref, va_ref, o_ref,
                key_ref, bias_ref, qs_ref, m_ref, acc_ref, xsel_ref,
                *, tq, k_top):
    j = pl.program_id(1)
    nkb = j + 1
    tk = tq
    nh = H_A
    groups = tk // SUBLANES

    lane_q = lax.broadcasted_iota(jnp.int32, (tq, LANES), 1)
    lo_half = lane_q < 64

    qi = qi_ref[...]
    qi_heads = []
    for h in range(H_I):
        pair = qi[:, (h // 2) * LANES:(h // 2 + 1) * LANES]
        keep = lo_half if h % 2 == 0 else jnp.logical_not(lo_half)
        qi_heads.append(jnp.where(keep, pair, jnp.zeros_like(pair)))
    wi = wi_ref[...]
    t_idx = j * tq + lax.broadcasted_iota(jnp.int32, (tk, tq), 1)
    s_loc = lax.broadcasted_iota(jnp.int32, (tk, tq), 0)

    def score_body(kb, carry):
        kblk = ki_ref[pl.ds(pl.multiple_of(kb * tk, tk), tk), :]
        acc = jnp.zeros((tk, tq), F32)
        for h in range(H_I):
            acc = acc + wi[h:h + 1, :] * jnp.maximum(_nt_dot(kblk, qi_heads[h]), 0.0)
        key = jnp.where(kb * tk + s_loc <= t_idx, _sortable_key(acc), INT_MIN)
        key_ref[kb] = key.reshape(groups, SUBLANES, tq)
        return carry

    lax.fori_loop(0, nkb, score_body, 0)

    ways = 4

    def count(pred):
        def body(kb, acc):
            hit = pred(key_ref[kb], kb * tk).astype(jnp.int32)
            part = groups // ways
            sums = [jnp.sum(hit[w * part:(w + 1) * part], axis=0) for w in range(ways)]
            return acc + ((sums[0] + sums[1]) + (sums[2] + sums[3]))
        acc = lax.fori_loop(0, nkb, body, jnp.zeros((SUBLANES, tq), jnp.int32))
        return jnp.sum(acc, axis=0, keepdims=True)

    def bcast8(v):
        return jnp.broadcast_to(v, (SUBLANES, tq))[None]

    def thr_body(step, thr):
        bit = jnp.where(step == 0, 0, jnp.left_shift(1, jnp.maximum(31 - step, 0)))
        cand = jnp.where(step == 0, jnp.zeros_like(thr), thr | bit)
        c8 = bcast8(cand)
        n_ge = count(lambda blk, r0: blk >= c8)
        return jnp.where(n_ge >= k_top, cand, thr)

    thr = lax.fori_loop(0, 32, thr_body, jnp.full((1, tq), INT_MIN, jnp.int32))
    thr8 = bcast8(thr)
    n_gt = count(lambda blk, r0: blk > thr8)
    n_eq = count(lambda blk, r0: blk == thr8)
    need = k_top - n_gt

    xsel_ref[...] = jnp.full((SUBLANES, tq), nkb * tk, jnp.int32)
    tie_overflow = jnp.max(jnp.where((n_eq > need) & (thr != INT_MIN), 1, 0)) > 0

    @pl.when(tie_overflow)
    def _():
        row_in_chunk = (lax.broadcasted_iota(jnp.int32, (groups, SUBLANES, tq), 0) * SUBLANES
                        + lax.broadcasted_iota(jnp.int32, (groups, SUBLANES, tq), 1))
        nbits = max(1, (key_ref.shape[0] * tk - 1).bit_length())

        def idx_body(step, x):
            cand = x | jnp.left_shift(1, nbits - 1 - step)
            c8 = bcast8(cand)
            n_before = count(lambda blk, r0: (blk == thr8) & (row_in_chunk + r0 < c8))
            return jnp.where(n_before < need, cand, x)

        x = lax.fori_loop(0, nbits, idx_body, jnp.zeros((1, tq), jnp.int32))
        xsel_ref[...] = jnp.broadcast_to(x + 1, (SUBLANES, tq))

    xsel = xsel_ref[0:1, :]

    def bias_body(kb, carry):
        key = key_ref[kb].reshape(tk, tq)
        s_idx = kb * tk + s_loc
        sel = (key > thr) | ((key == thr) & (s_idx < xsel))
        sel = sel & (s_idx <= t_idx)
        bias_ref[kb] = jnp.where(sel, 0.0, NEG_MASKED).astype(F32).T
        return carry

    lax.fori_loop(0, nkb, bias_body, 0)

    qa = qa_ref[...]
    for h in range(nh):
        pair = qa[:, (h // 2) * LANES:(h // 2 + 1) * LANES]
        keep = lo_half if h % 2 == 0 else jnp.logical_not(lo_half)
        qs_ref[h * tq:(h + 1) * tq, :] = jnp.where(keep, pair, jnp.zeros_like(pair))
    m_ref[...] = jnp.full(m_ref.shape, NEG_INIT, F32)
    acc_ref[...] = jnp.zeros(acc_ref.shape, F32)

    def attn_body(kb, carry):
        k0 = pl.multiple_of(kb * tk, tk)
        kblk = ka_ref[pl.ds(k0, tk), :]
        vblk = va_ref[pl.ds(k0, tk), :]
        for h in range(nh):
            rows = slice(h * tq, (h + 1) * tq)
            s = _nt_dot(qs_ref[rows, :], kblk) + bias_ref[kb]
            m_prev = m_ref[rows, :]
            m_new = jnp.maximum(m_prev, jnp.max(s, axis=1, keepdims=True))
            p = jnp.exp2(s - jnp.tile(m_new, (1, tk // LANES)))
            acc_ref[rows, :] = acc_ref[rows, :] * jnp.exp2(m_prev - m_new) + _dot(p.astype(BF16), vblk)
            m_ref[rows, :] = m_new
        return carry

    lax.fori_loop(0, nkb, attn_body, 0)

    for pr in range(nh // 2):
        outs = []
        for h in (2 * pr, 2 * pr + 1):
            acc = acc_ref[h * tq:(h + 1) * tq, :]
            outs.append(acc / pltpu.roll(acc, 64, 1))
        merged = jnp.where(lo_half, outs[0], pltpu.roll(outs[1], 64, 1))
        o_ref[:, pr * LANES:(pr + 1) * LANES] = merged.astype(BF16)


def _dsa(qa, qi, wi_t, ki, ka, va, tq):
    b, l, _ = qa.shape
    nq = l // tq
    k_top = min(TOPK_MAX, l // 4)
    kern = functools.partial(_dsa_kernel, tq=tq, k_top=k_top)
    per_q = lambda w: pl.BlockSpec((None, tq, w), lambda bi, j: (bi, j, 0))
    per_b = pl.BlockSpec((None, l, LANES), lambda bi, j: (bi, 0, 0))
    return pl.pallas_call(
        kern, grid=(b, nq),
        in_specs=[per_q(H_A * D_A), per_q(H_I * D_I),
                  pl.BlockSpec((SUBLANES, tq), lambda bi, j: (0, bi * nq + j)),
                  per_b, per_b, per_b],
        out_specs=per_q(H_A * D_A),
        out_shape=jax.ShapeDtypeStruct((b, l, H_A * D_A), BF16),
        scratch_shapes=[
            pltpu.VMEM((nq, tq // SUBLANES, SUBLANES, tq), jnp.int32),
            pltpu.VMEM((nq, tq, tq), F32),
            pltpu.VMEM((H_A * tq, LANES), BF16),
            pltpu.VMEM((H_A * tq, LANES), F32),
            pltpu.VMEM((H_A * tq, LANES), F32),
            pltpu.VMEM((SUBLANES, tq), jnp.int32),
        ],
        compiler_params=_cparams(2), name="dsa")(qa, qi, wi_t, ki, ka, va)


def _diff_kernel(qb_ref, kb_ref, vb_ref, lam_ref, gsub_ref, o_ref, qs_ref, m_ref, l_ref, acc_ref,
                 *, tq, lam_init):
    j = pl.program_id(1)
    tk = tq
    lamv = lam_ref[...]
    lam = (jnp.exp(jnp.sum(lamv[0:1] * lamv[1:2], axis=1, keepdims=True))
           - jnp.exp(jnp.sum(lamv[2:3] * lamv[3:4], axis=1, keepdims=True)) + lam_init)
    lane_q = lax.broadcasted_iota(jnp.int32, (tq, LANES), 1)
    lo_half = lane_q < 64
    row = lax.broadcasted_iota(jnp.int32, (2 * tq, tk), 0)
    t_loc = jnp.where(row >= tq, row - tq, row)
    s_loc = lax.broadcasted_iota(jnp.int32, (2 * tq, tk), 1)
    causal_diag = s_loc <= t_loc

    for h in range(H_B):
        pair = qb_ref[:, h * LANES:(h + 1) * LANES]
        qs_ref[h, 0:tq, :] = jnp.where(lo_half, pair, jnp.zeros_like(pair))
        qs_ref[h, tq:2 * tq, :] = jnp.where(lo_half, jnp.zeros_like(pair), pair)
    m_ref[...] = jnp.full(m_ref.shape, NEG_INIT, F32)
    l_ref[...] = jnp.zeros(l_ref.shape, F32)
    acc_ref[...] = jnp.zeros(acc_ref.shape, F32)

    def step(kb, masked):
        k0 = pl.multiple_of(kb * tk, tk)
        for h in range(H_B):
            cols = slice(h * LANES, (h + 1) * LANES)
            s = _nt_dot(qs_ref[h], kb_ref[pl.ds(k0, tk), cols])
            if masked:
                s = jnp.where(causal_diag, s, NEG_MASKED)
            m_prev = m_ref[h]
            m_new = jnp.maximum(m_prev, jnp.max(s, axis=1, keepdims=True))
            alpha = jnp.exp2(m_prev - m_new)
            p = jnp.exp2(s - jnp.tile(m_new, (1, tk // LANES)))
            l_ref[h] = alpha * l_ref[h] + jnp.sum(p, axis=1, keepdims=True)
            acc_ref[h] = alpha * acc_ref[h] + _dot(p.astype(BF16), vb_ref[pl.ds(k0, tk), cols])
            m_ref[h] = m_new

    def body(kb, carry):
        step(kb, False)
        return carry

    lax.fori_loop(0, j, body, 0)
    step(j, True)

    for h in range(H_B):
        o = acc_ref[h] / l_ref[h]
        o = o[0:tq] - lam * o[tq:2 * tq]
        y = _rms_rows(o, gsub_ref[...]) * (1.0 - lam_init)
        o_ref[:, h * LANES:(h + 1) * LANES] = y.astype(BF16)


def _diffattn(qb, kb, vb, lam_p, gsub, tq, lam_init):
    b, l, w = qb.shape
    nq = l // tq
    kern = functools.partial(_diff_kernel, tq=tq, lam_init=lam_init)
    per_q = pl.BlockSpec((None, tq, w), lambda bi, j: (bi, j, 0))
    per_b = pl.BlockSpec((None, l, w), lambda bi, j: (bi, 0, 0))
    full = lambda a: pl.BlockSpec(a.shape, lambda bi, j: (0,) * a.ndim)
    return pl.pallas_call(
        kern, grid=(b, nq),
        in_specs=[per_q, per_b, per_b, full(lam_p), full(gsub)],
        out_specs=per_q,
        out_shape=jax.ShapeDtypeStruct((b, l, w), BF16),
        scratch_shapes=[pltpu.VMEM((H_B, 2 * tq, LANES), BF16),
                        pltpu.VMEM((H_B, 2 * tq, LANES), F32),
                        pltpu.VMEM((H_B, 2 * tq, LANES), F32),
                        pltpu.VMEM((H_B, 2 * tq, LANES), F32)],
        compiler_params=_cparams(2), name="diffattn")(qb, kb, vb, lam_p, gsub)


def _merge_kernel(h_ref, g_ref, oa_ref, ob_ref, wg_ref, wa_ref, wb_ref, wo_ref, out_ref):
    h = h_ref[...]
    d = h.shape[-1]
    u = _rms_rows(h, g_ref[...]).astype(BF16)
    gates = jax.nn.sigmoid(_dot(u, wg_ref[...]))
    mix = (gates[:, 0:d] * _dot(oa_ref[...], wa_ref[...])
           + gates[:, d:2 * d] * _dot(ob_ref[...], wb_ref[...]))
    out_ref[...] = h + _dot(mix.astype(BF16), wo_ref[...])


def _merge(h, g, oa, ob, wg, wa, wb, wo, tm):
    t, d = h.shape
    rows = lambda w: pl.BlockSpec((tm, w), lambda i: (i, 0))
    full = lambda a: pl.BlockSpec(a.shape, lambda i: (0,) * a.ndim)
    return pl.pallas_call(
        _merge_kernel, grid=(t // tm,),
        in_specs=[rows(d), full(g), rows(oa.shape[1]), rows(ob.shape[1]), full(wg), full(wa), full(wb), full(wo)],
        out_specs=rows(d), out_shape=jax.ShapeDtypeStruct((t, d), F32),
        compiler_params=_cparams(1), name="merge")(h, g, oa, ob, wg, wa, wb, wo)


def _gelu_tanh(x):
    return 0.5 * x * (1.0 + jnp.tanh(math.sqrt(2.0 / math.pi) * (x + 0.044715 * (x * x * x))))


def _ffn_kernel(h_ref, g_ref, wu_ref, cp_ref, wd_ref, out_ref, tail_ref, acc_ref, *, tiles_per_seq, cw):
    i = pl.program_id(0)
    tm = h_ref.shape[0]
    d_ff = wd_ref.shape[0]
    h = h_ref[...]
    u = _rms_rows(h, g_ref[...]).astype(BF16)
    seq_start = (i % tiles_per_seq) == 0
    row = lax.broadcasted_iota(jnp.int32, (tm, 1), 0)
    acc_ref[...] = jnp.zeros(acc_ref.shape, F32)

    def conv_cols(col0):
        cols = pl.ds(pl.multiple_of(col0, LANES), cw)
        x = _dot(u, wu_ref[:, cols])
        taps = cp_ref[:, cols]
        prev = jnp.where(seq_start, 0.0, tail_ref[:, cols])
        tail_ref[:, cols] = x[tm - SUBLANES:tm, :]
        x1 = pltpu.roll(x, 1, 0)
        x2 = pltpu.roll(x, 2, 0)
        x1 = jnp.where(row == 0, prev[7:8, :], x1)
        x2 = jnp.where(row == 0, prev[6:7, :], jnp.where(row == 1, prev[7:8, :], x2))
        return taps[3:4, :] + x2 * taps[0:1, :] + x1 * taps[1:2, :] + x * taps[2:3, :]

    def chunk(c, carry):
        gate = conv_cols(c * cw)
        val = conv_cols(d_ff + c * cw)
        act = (_gelu_tanh(gate) * val).astype(BF16)
        acc_ref[...] += _dot(act, wd_ref[pl.ds(pl.multiple_of(c * cw, cw), cw), :])
        return carry

    lax.fori_loop(0, d_ff // cw, chunk, 0)
    out_ref[...] = h + acc_ref[...]


def _convffn(h, g, wu, cp, wd, seq_len, tm, cw):
    t, d = h.shape
    rows = pl.BlockSpec((tm, d), lambda i: (i, 0))
    full = lambda a: pl.BlockSpec(a.shape, lambda i: (0,) * a.ndim)
    kern = functools.partial(_ffn_kernel, tiles_per_seq=seq_len // tm, cw=cw)
    return pl.pallas_call(
        kern, grid=(t // tm,),
        in_specs=[rows, full(g), full(wu), full(cp), full(wd)],
        out_specs=rows, out_shape=jax.ShapeDtypeStruct((t, d), F32),
        scratch_shapes=[pltpu.VMEM((SUBLANES, wu.shape[1]), F32), pltpu.VMEM((tm, d), F32)],
        compiler_params=_cparams(1), name="convffn")(h, g, wu, cp, wd)


def _ple_kernel(h_ref, g_ref, p_ref, wg_ref, wp_ref, out_ref):
    h = h_ref[...]
    u = _rms_rows(h, g_ref[...]).astype(BF16)
    gate = jax.nn.sigmoid(_dot(u, wg_ref[...]))
    out_ref[...] = h + gate * _dot(p_ref[...].astype(BF16), wp_ref[...])


def _ple(h, g, p, wg, wp, tm):
    t, d = h.shape
    rows = lambda w: pl.BlockSpec((tm, w), lambda i: (i, 0))
    full = lambda a: pl.BlockSpec(a.shape, lambda i: (0,) * a.ndim)
    return pl.pallas_call(
        _ple_kernel, grid=(t // tm,),
        in_specs=[rows(d), full(g), rows(p.shape[1]), full(wg), full(wp)],
        out_specs=rows(d), out_shape=jax.ShapeDtypeStruct((t, d), F32),
        compiler_params=_cparams(1), name="ple")(h, g, p, wg, wp)


def _rope_tables(length):
    inv = 1.0 / (ROPE_THETA ** (jnp.arange(0, D_A, 2, dtype=F32) / D_A))
    ang = jnp.arange(length, dtype=F32)[:, None] * inv[None, :]
    c, s = jnp.cos(ang), jnp.sin(ang)
    return jnp.tile(c, (1, 4)), jnp.tile(jnp.concatenate([-s, s], axis=1), (1, 2))


def _pick_tile(n, pref):
    tile = min(n, pref)
    assert n % tile == 0, (n, tile)
    return tile


def kernel(x, p, g_mix_norm, w_in, g_qa, g_ka, g_qb, g_kb, lam_q1, lam_k1, lam_q2, lam_k2, g_subln,
           w_branch_a, w_branch_b, w_out, g_ffn_norm, w_up, conv_w, conv_b, w_down, g_ple_norm,
           w_ple_gate, w_ple_proj):
    bsz, seq_len, d_model = x.shape
    depth = w_in.shape[0]
    d_ff = w_down.shape[1]
    t = bsz * seq_len
    tq = _pick_tile(seq_len, 256)
    tm = _pick_tile(seq_len, 512)
    cw = 256
    assert d_ff % cw == 0

    cos_t, sin_t = _rope_tables(seq_len)
    gsum = jnp.kron(jnp.eye(8, dtype=F32), jnp.ones((64, 64), F32)).astype(BF16)
    tabs = {"cos": cos_t, "sin": sin_t, "gsum": gsum}

    sizes = [H_A * D_A, D_A, D_A, H_I * D_I, D_I, H_I, 2 * H_B * D_B, 2 * H_B * D_B, 2 * H_B * D_B, d_model, d_model]
    offs = [0]
    for sz in sizes:
        offs.append(offs[-1] + sz)
    assert offs[-1] == w_in.shape[2]

    h = x.reshape(t, d_model)
    for i in range(depth):
        lam_init = 0.8 - 0.6 * math.exp(-0.3 * i)
        wi_full = w_in[i]
        seg = lambda k: wi_full[:, offs[k]:offs[k + 1]]
        wts = {
            "qa": seg(0).astype(BF16),
            "sm": jnp.concatenate([seg(1), seg(1), seg(2), seg(2), seg(4), seg(4)], axis=1).astype(BF16),
            "qi": seg(3).astype(BF16),
            "wi": jnp.pad(seg(5).T, ((0, SUBLANES - H_I), (0, 0))).astype(BF16),
            "qb": seg(6).astype(BF16), "kb": seg(7).astype(BF16), "vb": seg(8).astype(BF16),
            "gqa": jnp.tile(g_qa[i], H_A)[None, :], "gka": jnp.tile(g_ka[i], 2)[None, :],
            "gqb": jnp.tile(g_qb[i], 2 * H_B)[None, :], "gkb": jnp.tile(g_kb[i], 2 * H_B)[None, :],
        }
        qa, ka, va, qi, ki, wi_t, qb, kb, vb = _in_proj(h, g_mix_norm[i][None, :], wts, tabs, seq_len, tm)
        r3 = lambda a: a.reshape(bsz, seq_len, a.shape[-1])
        o_a = _dsa(r3(qa), r3(qi), wi_t, r3(ki), r3(ka), r3(va), tq)
        lam_p = jnp.stack([lam_q1[i], lam_k1[i], lam_q2[i], lam_k2[i]])
        o_b = _diffattn(r3(qb), r3(kb), r3(vb), lam_p, g_subln[i][None, :], tq, lam_init)

        w_gate = jnp.concatenate([seg(9), seg(10)], axis=1).astype(BF16)
        h = _merge(h, g_mix_norm[i][None, :], o_a.reshape(t, -1), o_b.reshape(t, -1), w_gate,
                   w_branch_a[i].astype(BF16), w_branch_b[i].astype(BF16), w_out[i].astype(BF16), tm)

        conv = jnp.concatenate([conv_w[i], conv_b[i][None, :]], axis=0)
        conv = jnp.pad(conv, ((0, SUBLANES - CONV_W - 1), (0, 0)))
        h = _convffn(h, g_ffn_norm[i][None, :], w_up[i].astype(BF16), conv, w_down[i].astype(BF16),
                     seq_len, tm, cw)

        h = _ple(h, g_ple_norm[i][None, :], p[i].reshape(t, -1), w_ple_gate[i].astype(BF16),
                 w_ple_proj[i].astype(BF16), tm)
    return h.reshape(bsz, seq_len, d_model)
```

```python
import functools
import math

import jax
import jax.numpy as jnp
from jax import lax
from jax.experimental import pallas as pl
from jax.experimental.pallas import tpu as pltpu

H_A, D_A = 8, 64
H_I, D_I = 4, 64
H_B, D_B = 4, 64
TOPK_MAX = 256
CONV_W = 3
ROPE_THETA = 10000.0
EPS = 1e-6

LANES = 128
SUBLANES = 8
VMEM_LIMIT = 56 * 1024 * 1024

F32 = jnp.float32
BF16 = jnp.bfloat16
NEG_MASKED = -2e30
NEG_INIT = -1e30
INT_MIN = -(2 ** 31)
LOG2E = math.log2(math.e)


def _nt_dot(a, b):
    return lax.dot_general(a, b, (((1,), (1,)), ((), ())), preferred_element_type=F32)


def _dot(a, b):
    return jnp.dot(a, b, preferred_element_type=F32)


def _cparams(n_axes):
    return pltpu.CompilerParams(dimension_semantics=("arbitrary",) * n_axes,
                                vmem_limit_bytes=VMEM_LIMIT)


def _rms_rows(x, g):
    ms = jnp.mean(x * x, axis=-1, keepdims=True)
    return x * lax.rsqrt(ms + EPS) * g


def _rope(x, cos_t, sin_t):
    w = x.shape[-1]
    reps = w // LANES
    c = jnp.tile(cos_t, (1, reps)) if reps > 1 else cos_t
    s = jnp.tile(sin_t, (1, reps)) if reps > 1 else sin_t
    up = pltpu.roll(x, w - 32, 1)
    dn = pltpu.roll(x, 32, 1)
    lane = lax.broadcasted_iota(jnp.int32, x.shape, 1)
    partner = jnp.where((lane & 63) < 32, up, dn)
    return x * c + partner * s


def _head_norm(x, gsum, g):
    ss = _dot((x * x).astype(BF16), gsum) * (1.0 / 64.0)
    return x * lax.rsqrt(ss + EPS) * g


def _in_proj_kernel(h_ref, g_ref, wqa_ref, wsm_ref, wqi_ref, wqb_ref, wkb_ref, wvb_ref,
                    cos_ref, sin_ref, gsum_ref, gqa_ref, gka_ref, gqb_ref, gkb_ref,
                    qa_ref, ka_ref, va_ref, qi_ref, ki_ref, wi_ref, qb_ref, kb_ref, vb_ref):
    u = _rms_rows(h_ref[...], g_ref[...]).astype(BF16)
    cos_t, sin_t = cos_ref[...], sin_ref[...]
    gsum = gsum_ref[...]

    qa = _rope(_head_norm(_dot(u, wqa_ref[...]), gsum, gqa_ref[...]), cos_t, sin_t)
    qa_ref[...] = (qa * (D_A ** -0.5 * LOG2E)).astype(BF16)

    sm = _dot(u, wsm_ref[...])
    ka = _rope(_head_norm(sm[:, 0:LANES], gsum[0:LANES, 0:LANES], gka_ref[...]), cos_t, sin_t)
    ka_ref[...] = ka.astype(BF16)
    lane = lax.broadcasted_iota(jnp.int32, (sm.shape[0], LANES), 1)
    va_ref[...] = jnp.where(lane < D_A, sm[:, LANES:2 * LANES], 1.0).astype(BF16)
    ki_ref[...] = _rope(sm[:, 2 * LANES:3 * LANES], cos_t, sin_t).astype(BF16)

    qi_ref[...] = _rope(_dot(u, wqi_ref[...]), cos_t, sin_t).astype(BF16)
    wi_t = sm[:, 3 * LANES:4 * LANES].T
    wi_ref[...] = wi_t[0:SUBLANES, :] * (H_I ** -0.5 * D_I ** -0.5)

    qb = _rope(_head_norm(_dot(u, wqb_ref[...]), gsum, gqb_ref[...]), cos_t, sin_t)
    qb_ref[...] = (qb * (D_B ** -0.5 * LOG2E)).astype(BF16)
    kb = _rope(_head_norm(_dot(u, wkb_ref[...]), gsum, gkb_ref[...]), cos_t, sin_t)
    kb_ref[...] = kb.astype(BF16)
    vb_ref[...] = _dot(u, wvb_ref[...]).astype(BF16)


def _in_proj(h, g, w, tabs, seq_len, tm):
    t, d = h.shape
    nseq = seq_len // tm

    def full(a):
        return pl.BlockSpec(a.shape, lambda i: (0,) * a.ndim)

    def rows(width):
        return pl.BlockSpec((tm, width), lambda i: (i, 0))

    tab = pl.BlockSpec((tm, LANES), lambda i: (i % nseq, 0))
    ins = [h, g, w["qa"], w["sm"], w["qi"], w["qb"], w["kb"], w["vb"],
           tabs["cos"], tabs["sin"], tabs["gsum"], w["gqa"], w["gka"], w["gqb"], w["gkb"]]
    in_specs = [rows(d), full(g)] + [full(a) for a in ins[2:8]] + [tab, tab] + [full(a) for a in ins[10:]]
    widths = [H_A * D_A, LANES, LANES, H_I * D_I, LANES, None, 2 * H_B * D_B, 2 * H_B * D_B, 2 * H_B * D_B]
    out_shape, out_specs = [], []
    for wd in widths:
        if wd is None:
            out_shape.append(jax.ShapeDtypeStruct((SUBLANES, t), F32))
            out_specs.append(pl.BlockSpec((SUBLANES, tm), lambda i: (0, i)))
        else:
            out_shape.append(jax.ShapeDtypeStruct((t, wd), BF16))
            out_specs.append(rows(wd))
    return pl.pallas_call(
        _in_proj_kernel, grid=(t // tm,), in_specs=in_specs, out_specs=out_specs, out_shape=out_shape,
        compiler_params=_cparams(1), name="in_proj")(*ins)


def _key_to_float(k):
    return lax.bitcast_convert_type(k ^ ((k >> 31) & 0x7FFFFFFF), F32)


def _dsa_kernel(qa_ref, qi_ref, wi_ref, ki_ref, ka_ref, va_ref, ltri_ref, o_ref,
                score_ref, bias_ref, qs_ref, m_ref, acc_ref,
                *, tq, k_top):
    j = pl.program_id(1)
    nkb = j + 1
    tk = tq
    nh = H_A
    groups = tk // SUBLANES
    no_limit = 2 ** 30

    lane_q = lax.broadcasted_iota(jnp.int32, (tq, LANES), 1)
    lo_half = lane_q < 64

    qi = qi_ref[...]
    qi_heads = []
    for h in range(H_I):
        pair = qi[:, (h // 2) * LANES:(h // 2 + 1) * LANES]
        keep = lo_half if h % 2 == 0 else jnp.logical_not(lo_half)
        qi_heads.append(jnp.where(keep, pair, jnp.zeros_like(pair)))
    wi = wi_ref[...]
    t_idx = j * tq + lax.broadcasted_iota(jnp.int32, (tk, tq), 1)
    s_loc = lax.broadcasted_iota(jnp.int32, (tk, tq), 0)

    def score_body(kb, carry):
        kblk = ki_ref[pl.ds(pl.multiple_of(kb * tk, tk), tk), :]
        acc = jnp.zeros((tk, tq), F32)
        for h in range(H_I):
            acc = acc + wi[h:h + 1, :] * jnp.maximum(_nt_dot(kblk, qi_heads[h]), 0.0)
        sc = jnp.where(kb * tk + s_loc <= t_idx, acc, -jnp.inf)
        score_ref[kb] = sc.reshape(groups, SUBLANES, tq)
        return carry

    lax.fori_loop(0, nkb, score_body, 0)

    ways = 4

    def count(pred):
        def body(kb, acc):
            hit = pred(score_ref[kb]).astype(jnp.int32)
            part = groups // ways
            sums = [jnp.sum(hit[w * part:(w + 1) * part], axis=0) for w in range(ways)]
            return acc + ((sums[0] + sums[1]) + (sums[2] + sums[3]))
        acc = lax.fori_loop(0, nkb, body, jnp.zeros((SUBLANES, tq), jnp.int32))
        return jnp.sum(acc, axis=0, keepdims=True)

    def bcast8(v):
        return jnp.broadcast_to(v, (SUBLANES, tq))[None]

    def search_body(step, thr_key):
        cand = jnp.where(step == 0, 0, thr_key | jnp.left_shift(1, jnp.maximum(31 - step, 0)))
        c8 = bcast8(_key_to_float(cand))
        n_ge = count(lambda blk: blk >= c8)
        return jnp.where(n_ge >= k_top, cand, thr_key)

    thr_key = lax.fori_loop(0, 32, search_body, jnp.full((1, tq), INT_MIN, jnp.int32))
    n_valid = j * tq + lax.broadcasted_iota(jnp.int32, (1, tq), 1) + 1
    take_all = n_valid <= k_top
    thr = jnp.where(take_all, -jnp.inf, _key_to_float(thr_key))
    t8 = bcast8(thr)
    n_gt = count(lambda blk: blk > t8)
    need = jnp.where(take_all, no_limit, k_top - n_gt).astype(F32)

    def bias_body(kb, ties_before):
        sc = score_ref[kb].reshape(tk, tq)
        eq = sc == thr
        incl = _dot(ltri_ref[...], jnp.where(eq, 1.0, 0.0).astype(BF16))
        sel = (sc > thr) | (eq & (incl + ties_before <= need))
        sel = sel & (kb * tk + s_loc <= t_idx)
        bias_ref[kb] = jnp.where(sel, 0.0, NEG_MASKED).astype(F32).T
        return ties_before + incl[tk - 1:tk, :]

    lax.fori_loop(0, nkb, bias_body, jnp.zeros((1, tq), F32))

    qa = qa_ref[...]
    for h in range(nh):
        pair = qa[:, (h // 2) * LANES:(h // 2 + 1) * LANES]
        keep = lo_half if h % 2 == 0 else jnp.logical_not(lo_half)
        qs_ref[h * tq:(h + 1) * tq, :] = jnp.where(keep, pair, jnp.zeros_like(pair))
    m_ref[...] = jnp.full(m_ref.shape, NEG_INIT, F32)
    acc_ref[...] = jnp.zeros(acc_ref.shape, F32)

    def attn_body(kb, carry):
        k0 = pl.multiple_of(kb * tk, tk)
        kblk = ka_ref[pl.ds(k0, tk), :]
        vblk = va_ref[pl.ds(k0, tk), :]
        for h in range(nh):
            rows = slice(h * tq, (h + 1) * tq)
            s = _nt_dot(qs_ref[rows, :], kblk) + bias_ref[kb]
            m_prev = m_ref[rows, :]
            m_new = jnp.maximum(m_prev, jnp.max(s, axis=1, keepdims=True))
            p = jnp.exp2(s - jnp.tile(m_new, (1, tk // LANES)))
            acc_ref[rows, :] = acc_ref[rows, :] * jnp.exp2(m_prev - m_new) + _dot(p.astype(BF16), vblk)
            m_ref[rows, :] = m_new
        return carry

    lax.fori_loop(0, nkb, attn_body, 0)

    for pr in range(nh // 2):
        outs = []
        for h in (2 * pr, 2 * pr + 1):
            acc = acc_ref[h * tq:(h + 1) * tq, :]
            outs.append(acc / pltpu.roll(acc, 64, 1))
        merged = jnp.where(lo_half, outs[0], pltpu.roll(outs[1], 64, 1))
        o_ref[:, pr * LANES:(pr + 1) * LANES] = merged.astype(BF16)


def _dsa(qa, qi, wi_t, ki, ka, va, tq):
    b, l, _ = qa.shape
    nq = l // tq
    k_top = min(TOPK_MAX, l // 4)
    kern = functools.partial(_dsa_kernel, tq=tq, k_top=k_top)
    per_q = lambda w: pl.BlockSpec((None, tq, w), lambda bi, j: (bi, j, 0))
    per_b = pl.BlockSpec((None, l, LANES), lambda bi, j: (bi, 0, 0))
    ltri = jnp.tril(jnp.ones((tq, tq), F32)).astype(BF16)
    return pl.pallas_call(
        kern, grid=(b, nq),
        in_specs=[per_q(H_A * D_A), per_q(H_I * D_I),
                  pl.BlockSpec((SUBLANES, tq), lambda bi, j: (0, bi * nq + j)),
                  per_b, per_b, per_b,
                  pl.BlockSpec((tq, tq), lambda bi, j: (0, 0))],
        out_specs=per_q(H_A * D_A),
        out_shape=jax.ShapeDtypeStruct((b, l, H_A * D_A), BF16),
        scratch_shapes=[
            pltpu.VMEM((nq, tq // SUBLANES, SUBLANES, tq), F32),
            pltpu.VMEM((nq, tq, tq), F32),
            pltpu.VMEM((H_A * tq, LANES), BF16),
            pltpu.VMEM((H_A * tq, LANES), F32),
            pltpu.VMEM((H_A * tq, LANES), F32),
        ],
        compiler_params=_cparams(2), name="dsa")(qa, qi, wi_t, ki, ka, va, ltri)


def _diff_kernel(qb_ref, kb_ref, vb_ref, lam_ref, gsub_ref, o_ref, qs_ref, m_ref, l_ref, acc_ref,
                 *, tq, lam_init):
    j = pl.program_id(1)
    tk = tq
    lamv = lam_ref[...]
    lam = (jnp.exp(jnp.sum(lamv[0:1] * lamv[1:2], axis=1, keepdims=True))
           - jnp.exp(jnp.sum(lamv[2:3] * lamv[3:4], axis=1, keepdims=True)) + lam_init)
    lane_q = lax.broadcasted_iota(jnp.int32, (tq, LANES), 1)
    lo_half = lane_q < 64
    row = lax.broadcasted_iota(jnp.int32, (2 * tq, tk), 0)
    t_loc = jnp.where(row >= tq, row - tq, row)
    s_loc = lax.broadcasted_iota(jnp.int32, (2 * tq, tk), 1)
    causal_diag = s_loc <= t_loc

    for h in range(H_B):
        pair = qb_ref[:, h * LANES:(h + 1) * LANES]
        qs_ref[h, 0:tq, :] = jnp.where(lo_half, pair, jnp.zeros_like(pair))
        qs_ref[h, tq:2 * tq, :] = jnp.where(lo_half, jnp.zeros_like(pair), pair)
    m_ref[...] = jnp.full(m_ref.shape, NEG_INIT, F32)
    l_ref[...] = jnp.zeros(l_ref.shape, F32)
    acc_ref[...] = jnp.zeros(acc_ref.shape, F32)

    def step(kb, masked):
        k0 = pl.multiple_of(kb * tk, tk)
        for h in range(H_B):
            cols = slice(h * LANES, (h + 1) * LANES)
            s = _nt_dot(qs_ref[h], kb_ref[pl.ds(k0, tk), cols])
            if masked:
                s = jnp.where(causal_diag, s, NEG_MASKED)
            m_prev = m_ref[h]
            m_new = jnp.maximum(m_prev, jnp.max(s, axis=1, keepdims=True))
            alpha = jnp.exp2(m_prev - m_new)
            p = jnp.exp2(s - jnp.tile(m_new, (1, tk // LANES)))
            l_ref[h] = alpha * l_ref[h] + jnp.sum(p, axis=1, keepdims=True)
            acc_ref[h] = alpha * acc_ref[h] + _dot(p.astype(BF16), vb_ref[pl.ds(k0, tk), cols])
            m_ref[h] = m_new

    def body(kb, carry):
        step(kb, False)
        return carry

    lax.fori_loop(0, j, body, 0)
    step(j, True)

    for h in range(H_B):
        o = acc_ref[h] / l_ref[h]
        o = o[0:tq] - lam * o[tq:2 * tq]
        y = _rms_rows(o, gsub_ref[...]) * (1.0 - lam_init)
        o_ref[:, h * LANES:(h + 1) * LANES] = y.astype(BF16)


def _diffattn(qb, kb, vb, lam_p, gsub, tq, lam_init):
    b, l, w = qb.shape
    nq = l // tq
    kern = functools.partial(_diff_kernel, tq=tq, lam_init=lam_init)
    per_q = pl.BlockSpec((None, tq, w), lambda bi, j: (bi, j, 0))
    per_b = pl.BlockSpec((None, l, w), lambda bi, j: (bi, 0, 0))
    full = lambda a: pl.BlockSpec(a.shape, lambda bi, j: (0,) * a.ndim)
    return pl.pallas_call(
        kern, grid=(b, nq),
        in_specs=[per_q, per_b, per_b, full(lam_p), full(gsub)],
        out_specs=per_q,
        out_shape=jax.ShapeDtypeStruct((b, l, w), BF16),
        scratch_shapes=[pltpu.VMEM((H_B, 2 * tq, LANES), BF16),
                        pltpu.VMEM((H_B, 2 * tq, LANES), F32),
                        pltpu.VMEM((H_B, 2 * tq, LANES), F32),
                        pltpu.VMEM((H_B, 2 * tq, LANES), F32)],
        compiler_params=_cparams(2), name="diffattn")(qb, kb, vb, lam_p, gsub)


def _merge_kernel(h_ref, g_ref, oa_ref, ob_ref, wg_ref, wa_ref, wb_ref, wo_ref, out_ref):
    h = h_ref[...]
    d = h.shape[-1]
    u = _rms_rows(h, g_ref[...]).astype(BF16)
    gates = jax.nn.sigmoid(_dot(u, wg_ref[...]))
    mix = (gates[:, 0:d] * _dot(oa_ref[...], wa_ref[...])
           + gates[:, d:2 * d] * _dot(ob_ref[...], wb_ref[...]))
    out_ref[...] = h + _dot(mix.astype(BF16), wo_ref[...])


def _merge(h, g, oa, ob, wg, wa, wb, wo, tm):
    t, d = h.shape
    rows = lambda w: pl.BlockSpec((tm, w), lambda i: (i, 0))
    full = lambda a: pl.BlockSpec(a.shape, lambda i: (0,) * a.ndim)
    return pl.pallas_call(
        _merge_kernel, grid=(t // tm,),
        in_specs=[rows(d), full(g), rows(oa.shape[1]), rows(ob.shape[1]), full(wg), full(wa), full(wb), full(wo)],
        out_specs=rows(d), out_shape=jax.ShapeDtypeStruct((t, d), F32),
        compiler_params=_cparams(1), name="merge")(h, g, oa, ob, wg, wa, wb, wo)


def _gelu_tanh(x):
    return 0.5 * x * (1.0 + jnp.tanh(math.sqrt(2.0 / math.pi) * (x + 0.044715 * (x * x * x))))


def _ffn_kernel(h_ref, g_ref, wu_ref, cp_ref, wd_ref, out_ref, tail_ref, acc_ref, *, tiles_per_seq, cw):
    i = pl.program_id(0)
    tm = h_ref.shape[0]
    d_ff = wd_ref.shape[0]
    h = h_ref[...]
    u = _rms_rows(h, g_ref[...]).astype(BF16)
    seq_start = (i % tiles_per_seq) == 0
    row = lax.broadcasted_iota(jnp.int32, (tm, 1), 0)
    acc_ref[...] = jnp.zeros(acc_ref.shape, F32)

    def conv_cols(col0):
        cols = pl.ds(pl.multiple_of(col0, LANES), cw)
        x = _dot(u, wu_ref[:, cols])
        taps = cp_ref[:, cols]
        prev = jnp.where(seq_start, 0.0, tail_ref[:, cols])
        tail_ref[:, cols] = x[tm - SUBLANES:tm, :]
        x1 = pltpu.roll(x, 1, 0)
        x2 = pltpu.roll(x, 2, 0)
        x1 = jnp.where(row == 0, prev[7:8, :], x1)
        x2 = jnp.where(row == 0, prev[6:7, :], jnp.where(row == 1, prev[7:8, :], x2))
        return taps[3:4, :] + x2 * taps[0:1, :] + x1 * taps[1:2, :] + x * taps[2:3, :]

    def chunk(c, carry):
        gate = conv_cols(c * cw)
        val = conv_cols(d_ff + c * cw)
        act = (_gelu_tanh(gate) * val).astype(BF16)
        acc_ref[...] += _dot(act, wd_ref[pl.ds(pl.multiple_of(c * cw, cw), cw), :])
        return carry

    lax.fori_loop(0, d_ff // cw, chunk, 0)
    out_ref[...] = h + acc_ref[...]


def _convffn(h, g, wu, cp, wd, seq_len, tm, cw):
    t, d = h.shape
    rows = pl.BlockSpec((tm, d), lambda i: (i, 0))
    full = lambda a: pl.BlockSpec(a.shape, lambda i: (0,) * a.ndim)
    kern = functools.partial(_ffn_kernel, tiles_per_seq=seq_len // tm, cw=cw)
    return pl.pallas_call(
        kern, grid=(t // tm,),
        in_specs=[rows, full(g), full(wu), full(cp), full(wd)],
        out_specs=rows, out_shape=jax.ShapeDtypeStruct((t, d), F32),
        scratch_shapes=[pltpu.VMEM((SUBLANES, wu.shape[1]), F32), pltpu.VMEM((tm, d), F32)],
        compiler_params=_cparams(1), name="convffn")(h, g, wu, cp, wd)


def _ple_kernel(h_ref, g_ref, p_ref, wg_ref, wp_ref, out_ref):
    h = h_ref[...]
    u = _rms_rows(h, g_ref[...]).astype(BF16)
    gate = jax.nn.sigmoid(_dot(u, wg_ref[...]))
    out_ref[...] = h + gate * _dot(p_ref[...].astype(BF16), wp_ref[...])


def _ple(h, g, p, wg, wp, tm):
    t, d = h.shape
    rows = lambda w: pl.BlockSpec((tm, w), lambda i: (i, 0))
    full = lambda a: pl.BlockSpec(a.shape, lambda i: (0,) * a.ndim)
    return pl.pallas_call(
        _ple_kernel, grid=(t // tm,),
        in_specs=[rows(d), full(g), rows(p.shape[1]), full(wg), full(wp)],
        out_specs=rows(d), out_shape=jax.ShapeDtypeStruct((t, d), F32),
        compiler_params=_cparams(1), name="ple")(h, g, p, wg, wp)


def _rope_tables(length):
    inv = 1.0 / (ROPE_THETA ** (jnp.arange(0, D_A, 2, dtype=F32) / D_A))
    ang = jnp.arange(length, dtype=F32)[:, None] * inv[None, :]
    c, s = jnp.cos(ang), jnp.sin(ang)
    return jnp.tile(c, (1, 4)), jnp.tile(jnp.concatenate([-s, s], axis=1), (1, 2))


def _pick_tile(n, pref):
    tile = min(n, pref)
    assert n % tile == 0, (n, tile)
    return tile


def kernel(x, p, g_mix_norm, w_in, g_qa, g_ka, g_qb, g_kb, lam_q1, lam_k1, lam_q2, lam_k2, g_subln,
           w_branch_a, w_branch_b, w_out, g_ffn_norm, w_up, conv_w, conv_b, w_down, g_ple_norm,
           w_ple_gate, w_ple_proj):
    bsz, seq_len, d_model = x.shape
    depth = w_in.shape[0]
    d_ff = w_down.shape[1]
    t = bsz * seq_len
    tq = _pick_tile(seq_len, 256)
    tm = _pick_tile(seq_len, 512)
    cw = 256
    assert d_ff % cw == 0

    cos_t, sin_t = _rope_tables(seq_len)
    gsum = jnp.kron(jnp.eye(8, dtype=F32), jnp.ones((64, 64), F32)).astype(BF16)
    tabs = {"cos": cos_t, "sin": sin_t, "gsum": gsum}

    sizes = [H_A * D_A, D_A, D_A, H_I * D_I, D_I, H_I, 2 * H_B * D_B, 2 * H_B * D_B, 2 * H_B * D_B, d_model, d_model]
    offs = [0]
    for sz in sizes:
        offs.append(offs[-1] + sz)
    assert offs[-1] == w_in.shape[2]

    h = x.reshape(t, d_model)
    for i in range(depth):
        lam_init = 0.8 - 0.6 * math.exp(-0.3 * i)
        segs = lax.optimization_barrier([w_in[i, :, offs[k]:offs[k + 1]] for k in range(len(sizes))])
        seg = lambda k: segs[k]
        wts = {
            "qa": seg(0).astype(BF16),
            "sm": jnp.concatenate([seg(1), seg(1), seg(2), seg(2), seg(4), seg(4),
                                   jnp.pad(seg(5), ((0, 0), (0, LANES - H_I)))], axis=1).astype(BF16),
            "qi": seg(3).astype(BF16),
            "qb": seg(6).astype(BF16), "kb": seg(7).astype(BF16), "vb": seg(8).astype(BF16),
            "gqa": jnp.tile(g_qa[i], H_A)[None, :], "gka": jnp.tile(g_ka[i], 2)[None, :],
            "gqb": jnp.tile(g_qb[i], 2 * H_B)[None, :], "gkb": jnp.tile(g_kb[i], 2 * H_B)[None, :],
        }
        qa, ka, va, qi, ki, wi_t, qb, kb, vb = _in_proj(h, g_mix_norm[i][None, :], wts, tabs, seq_len, tm)
        r3 = lambda a: a.reshape(bsz, seq_len, a.shape[-1])
        o_a = _dsa(r3(qa), r3(qi), wi_t, r3(ki), r3(ka), r3(va), tq)
        lam_p = jnp.stack([lam_q1[i], lam_k1[i], lam_q2[i], lam_k2[i]])
        o_b = _diffattn(r3(qb), r3(kb), r3(vb), lam_p, g_subln[i][None, :], tq, lam_init)

        w_gate = jnp.concatenate([seg(9), seg(10)], axis=1).astype(BF16)
        h = _merge(h, g_mix_norm[i][None, :], o_a.reshape(t, -1), o_b.reshape(t, -1), w_gate,
                   w_branch_a[i].astype(BF16), w_branch_b[i].astype(BF16), w_out[i].astype(BF16), tm)

        conv = jnp.concatenate([conv_w[i], conv_b[i][None, :]], axis=0)
        conv = jnp.pad(conv, ((0, SUBLANES - CONV_W - 1), (0, 0)))
        h = _convffn(h, g_ffn_norm[i][None, :], w_up[i].astype(BF16), conv, w_down[i].astype(BF16),
                     seq_len, tm, cw)

        h = _ple(h, g_ple_norm[i][None, :], p[i].reshape(t, -1), w_ple_gate[i].astype(BF16),
                 w_ple_proj[i].astype(BF16), tm)
    return h.reshape(bsz, seq_len, d_model)
```

```python
import functools
import math

import jax
import jax.numpy as jnp
from jax import lax
from jax.experimental import pallas as pl
from jax.experimental.pallas import tpu as pltpu

H_A, D_A = 8, 64
H_I, D_I = 4, 64
H_B, D_B = 4, 64
TOPK_MAX = 256
CONV_W = 3
ROPE_THETA = 10000.0
EPS = 1e-6

LANES = 128
SUBLANES = 8
VMEM_LIMIT = 56 * 1024 * 1024

F32 = jnp.float32
BF16 = jnp.bfloat16
NEG_MASKED = -2e30
NEG_INIT = -1e30
INT_MIN = -(2 ** 31)
LOG2E = math.log2(math.e)


def _nt_dot(a, b):
    return lax.dot_general(a, b, (((1,), (1,)), ((), ())), preferred_element_type=F32)


def _dot(a, b):
    return jnp.dot(a, b, preferred_element_type=F32)


def _cparams(n_axes):
    return pltpu.CompilerParams(dimension_semantics=("arbitrary",) * n_axes,
                                vmem_limit_bytes=VMEM_LIMIT)


def _rms_rows(x, g):
    ms = jnp.mean(x * x, axis=-1, keepdims=True)
    return x * lax.rsqrt(ms + EPS) * g


def _rope(x, cos_t, sin_t):
    w = x.shape[-1]
    reps = w // LANES
    c = jnp.tile(cos_t, (1, reps)) if reps > 1 else cos_t
    s = jnp.tile(sin_t, (1, reps)) if reps > 1 else sin_t
    up = pltpu.roll(x, w - 32, 1)
    dn = pltpu.roll(x, 32, 1)
    lane = lax.broadcasted_iota(jnp.int32, x.shape, 1)
    partner = jnp.where((lane & 63) < 32, up, dn)
    return x * c + partner * s


def _head_norm(x, gsum, g):
    ss = _dot((x * x).astype(BF16), gsum) * (1.0 / 64.0)
    return x * lax.rsqrt(ss + EPS) * g


def _in_proj_kernel(h_ref, g_ref, wqa_ref, wsm_ref, wqi_ref, wqb_ref, wkb_ref, wvb_ref,
                    cos_ref, sin_ref, gsum_ref, gqa_ref, gka_ref, gqb_ref, gkb_ref,
                    qa_ref, ka_ref, va_ref, qi_ref, ki_ref, wi_ref, qb_ref, kb_ref, vb_ref):
    u = _rms_rows(h_ref[...], g_ref[...]).astype(BF16)
    cos_t, sin_t = cos_ref[...], sin_ref[...]
    gsum = gsum_ref[...]

    qa = _rope(_head_norm(_dot(u, wqa_ref[...]), gsum, gqa_ref[...]), cos_t, sin_t)
    qa_ref[...] = (qa * (D_A ** -0.5 * LOG2E)).astype(BF16)

    sm = _dot(u, wsm_ref[...])
    ka = _rope(_head_norm(sm[:, 0:LANES], gsum[0:LANES, 0:LANES], gka_ref[...]), cos_t, sin_t)
    ka_ref[...] = ka.astype(BF16)
    lane = lax.broadcasted_iota(jnp.int32, (sm.shape[0], LANES), 1)
    va_ref[...] = jnp.where(lane < D_A, sm[:, LANES:2 * LANES], 1.0).astype(BF16)
    ki_ref[...] = _rope(sm[:, 2 * LANES:3 * LANES], cos_t, sin_t).astype(BF16)

    qi_ref[...] = _rope(_dot(u, wqi_ref[...]), cos_t, sin_t).astype(BF16)
    wi_t = sm[:, 3 * LANES:4 * LANES].T
    wi_ref[...] = wi_t[0:SUBLANES, :] * (H_I ** -0.5 * D_I ** -0.5)

    qb = _rope(_head_norm(_dot(u, wqb_ref[...]), gsum, gqb_ref[...]), cos_t, sin_t)
    qb_ref[...] = (qb * (D_B ** -0.5 * LOG2E)).astype(BF16)
    kb = _rope(_head_norm(_dot(u, wkb_ref[...]), gsum, gkb_ref[...]), cos_t, sin_t)
    kb_ref[...] = kb.astype(BF16)
    vb_ref[...] = _dot(u, wvb_ref[...]).astype(BF16)


def _in_proj(h, g, w, tabs, seq_len, tm):
    t, d = h.shape
    nseq = seq_len // tm

    def full(a):
        return pl.BlockSpec(a.shape, lambda i: (0,) * a.ndim)

    def rows(width):
        return pl.BlockSpec((tm, width), lambda i: (i, 0))

    tab = pl.BlockSpec((tm, LANES), lambda i: (i % nseq, 0))
    ins = [h, g, w["qa"], w["sm"], w["qi"], w["qb"], w["kb"], w["vb"],
           tabs["cos"], tabs["sin"], tabs["gsum"], w["gqa"], w["gka"], w["gqb"], w["gkb"]]
    in_specs = [rows(d), full(g)] + [full(a) for a in ins[2:8]] + [tab, tab] + [full(a) for a in ins[10:]]
    widths = [H_A * D_A, LANES, LANES, H_I * D_I, LANES, None, 2 * H_B * D_B, 2 * H_B * D_B, 2 * H_B * D_B]
    out_shape, out_specs = [], []
    for wd in widths:
        if wd is None:
            out_shape.append(jax.ShapeDtypeStruct((SUBLANES, t), F32))
            out_specs.append(pl.BlockSpec((SUBLANES, tm), lambda i: (0, i)))
        else:
            out_shape.append(jax.ShapeDtypeStruct((t, wd), BF16))
            out_specs.append(rows(wd))
    return pl.pallas_call(
        _in_proj_kernel, grid=(t // tm,), in_specs=in_specs, out_specs=out_specs, out_shape=out_shape,
        compiler_params=_cparams(1), name="in_proj")(*ins)


def _key_to_float(k):
    return lax.bitcast_convert_type(k ^ ((k >> 31) & 0x7FFFFFFF), F32)


def _dsa_kernel(qa_ref, qi_ref, wi_ref, ki_ref, ka_ref, va_ref, ltri_ref, o_ref,
                score_ref, bias_ref, qs_ref, m_ref, acc_ref,
                *, tq, tk, k_top):
    j = pl.program_id(1)
    nkb = (j + 1) * (tq // tk)
    nh = H_A
    groups = tk // SUBLANES
    no_limit = 2 ** 30

    lane_q = lax.broadcasted_iota(jnp.int32, (tq, LANES), 1)
    lo_half = lane_q < 64

    qi = qi_ref[...]
    qi_heads = []
    for h in range(H_I):
        pair = qi[:, (h // 2) * LANES:(h // 2 + 1) * LANES]
        keep = lo_half if h % 2 == 0 else jnp.logical_not(lo_half)
        qi_heads.append(jnp.where(keep, pair, jnp.zeros_like(pair)))
    wi = wi_ref[...]
    t_idx = j * tq + lax.broadcasted_iota(jnp.int32, (tk, tq), 1)
    s_loc = lax.broadcasted_iota(jnp.int32, (tk, tq), 0)

    def score_body(kb, carry):
        kblk = ki_ref[pl.ds(pl.multiple_of(kb * tk, tk), tk), :]
        acc = jnp.zeros((tk, tq), F32)
        for h in range(H_I):
            acc = acc + wi[h:h + 1, :] * jnp.maximum(_nt_dot(kblk, qi_heads[h]), 0.0)
        sc = jnp.where(kb * tk + s_loc <= t_idx, acc, -jnp.inf)
        score_ref[kb] = sc.reshape(groups, SUBLANES, tq)
        return carry

    lax.fori_loop(0, nkb, score_body, 0)

    ways = 4

    def count(pred):
        def body(kb, acc):
            hit = pred(score_ref[kb]).astype(jnp.int32)
            part = groups // ways
            sums = [jnp.sum(hit[w * part:(w + 1) * part], axis=0) for w in range(ways)]
            return acc + ((sums[0] + sums[1]) + (sums[2] + sums[3]))
        acc = lax.fori_loop(0, nkb, body, jnp.zeros((SUBLANES, tq), jnp.int32))
        return jnp.sum(acc, axis=0, keepdims=True)

    def bcast8(v):
        return jnp.broadcast_to(v, (SUBLANES, tq))[None]

    def search_body(step, thr_key):
        cand = jnp.where(step == 0, 0, thr_key | jnp.left_shift(1, jnp.maximum(31 - step, 0)))
        c8 = bcast8(_key_to_float(cand))
        n_ge = count(lambda blk: blk >= c8)
        return jnp.where(n_ge >= k_top, cand, thr_key)

    thr_key = lax.fori_loop(0, 32, search_body, jnp.full((1, tq), INT_MIN, jnp.int32))
    n_valid = j * tq + lax.broadcasted_iota(jnp.int32, (1, tq), 1) + 1
    take_all = n_valid <= k_top
    thr = jnp.where(take_all, -jnp.inf, _key_to_float(thr_key))
    t8 = bcast8(thr)
    n_gt = count(lambda blk: blk > t8)
    need = jnp.where(take_all, no_limit, k_top - n_gt).astype(F32)

    def bias_body(kb, ties_before):
        sc = score_ref[kb].reshape(tk, tq)
        eq = sc == thr
        incl = _dot(ltri_ref[...], jnp.where(eq, 1.0, 0.0).astype(BF16))
        sel = (sc > thr) | (eq & (incl + ties_before <= need))
        sel = sel & (kb * tk + s_loc <= t_idx)
        bias_ref[kb] = jnp.where(sel, 0.0, NEG_MASKED).astype(F32).T
        return ties_before + incl[tk - 1:tk, :]

    lax.fori_loop(0, nkb, bias_body, jnp.zeros((1, tq), F32))

    qa = qa_ref[...]
    for h in range(nh):
        pair = qa[:, (h // 2) * LANES:(h // 2 + 1) * LANES]
        keep = lo_half if h % 2 == 0 else jnp.logical_not(lo_half)
        qs_ref[h * tq:(h + 1) * tq, :] = jnp.where(keep, pair, jnp.zeros_like(pair))
    m_ref[...] = jnp.full(m_ref.shape, NEG_INIT, F32)
    acc_ref[...] = jnp.zeros(acc_ref.shape, F32)

    def attn_body(kb, carry):
        k0 = pl.multiple_of(kb * tk, tk)
        kblk = ka_ref[pl.ds(k0, tk), :]
        vblk = va_ref[pl.ds(k0, tk), :]
        for h in range(nh):
            rows = slice(h * tq, (h + 1) * tq)
            s = _nt_dot(qs_ref[rows, :], kblk) + bias_ref[kb]
            m_prev = m_ref[rows, :]
            m_new = jnp.maximum(m_prev, jnp.max(s, axis=1, keepdims=True))
            p = jnp.exp2(s - jnp.tile(m_new, (1, tk // LANES)))
            acc_ref[rows, :] = acc_ref[rows, :] * jnp.exp2(m_prev - m_new) + _dot(p.astype(BF16), vblk)
            m_ref[rows, :] = m_new
        return carry

    lax.fori_loop(0, nkb, attn_body, 0)

    for pr in range(nh // 2):
        outs = []
        for h in (2 * pr, 2 * pr + 1):
            acc = acc_ref[h * tq:(h + 1) * tq, :]
            outs.append(acc / pltpu.roll(acc, 64, 1))
        merged = jnp.where(lo_half, outs[0], pltpu.roll(outs[1], 64, 1))
        o_ref[:, pr * LANES:(pr + 1) * LANES] = merged.astype(BF16)


def _dsa(qa, qi, wi_t, ki, ka, va, tq, tk):
    b, l, _ = qa.shape
    nq = l // tq
    k_top = min(TOPK_MAX, l // 4)
    kern = functools.partial(_dsa_kernel, tq=tq, tk=tk, k_top=k_top)
    per_q = lambda w: pl.BlockSpec((None, tq, w), lambda bi, j: (bi, j, 0))
    per_b = pl.BlockSpec((None, l, LANES), lambda bi, j: (bi, 0, 0))
    ltri = jnp.tril(jnp.ones((tk, tk), F32)).astype(BF16)
    return pl.pallas_call(
        kern, grid=(b, nq),
        in_specs=[per_q(H_A * D_A), per_q(H_I * D_I),
                  pl.BlockSpec((SUBLANES, tq), lambda bi, j: (0, bi * nq + j)),
                  per_b, per_b, per_b,
                  pl.BlockSpec((tk, tk), lambda bi, j: (0, 0))],
        out_specs=per_q(H_A * D_A),
        out_shape=jax.ShapeDtypeStruct((b, l, H_A * D_A), BF16),
        scratch_shapes=[
            pltpu.VMEM((l // tk, tk // SUBLANES, SUBLANES, tq), F32),
            pltpu.VMEM((l // tk, tq, tk), F32),
            pltpu.VMEM((H_A * tq, LANES), BF16),
            pltpu.VMEM((H_A * tq, LANES), F32),
            pltpu.VMEM((H_A * tq, LANES), F32),
        ],
        compiler_params=_cparams(2), name="dsa")(qa, qi, wi_t, ki, ka, va, ltri)


def _diff_kernel(qb_ref, kb_ref, vb_ref, lam_ref, gsub_ref, o_ref, qs_ref, m_ref, l_ref, acc_ref,
                 *, tq, tk, lam_init):
    j = pl.program_id(1)
    diag_blocks = tq // tk
    lamv = lam_ref[...]
    lam = (jnp.exp(jnp.sum(lamv[0:1] * lamv[1:2], axis=1, keepdims=True))
           - jnp.exp(jnp.sum(lamv[2:3] * lamv[3:4], axis=1, keepdims=True)) + lam_init)
    lane_q = lax.broadcasted_iota(jnp.int32, (tq, LANES), 1)
    lo_half = lane_q < 64
    row = lax.broadcasted_iota(jnp.int32, (2 * tq, tk), 0)
    t_loc = jnp.where(row >= tq, row - tq, row)
    s_loc = lax.broadcasted_iota(jnp.int32, (2 * tq, tk), 1)

    for h in range(H_B):
        pair = qb_ref[:, h * LANES:(h + 1) * LANES]
        qs_ref[h, 0:tq, :] = jnp.where(lo_half, pair, jnp.zeros_like(pair))
        qs_ref[h, tq:2 * tq, :] = jnp.where(lo_half, jnp.zeros_like(pair), pair)
    m_ref[...] = jnp.full(m_ref.shape, NEG_INIT, F32)
    l_ref[...] = jnp.zeros(l_ref.shape, F32)
    acc_ref[...] = jnp.zeros(acc_ref.shape, F32)

    def step(kb, diag):
        k0 = pl.multiple_of(kb * tk, tk)
        for h in range(H_B):
            cols = slice(h * LANES, (h + 1) * LANES)
            s = _nt_dot(qs_ref[h], kb_ref[pl.ds(k0, tk), cols])
            if diag is not None:
                s = jnp.where(diag * tk + s_loc <= t_loc, s, NEG_MASKED)
            m_prev = m_ref[h]
            m_new = jnp.maximum(m_prev, jnp.max(s, axis=1, keepdims=True))
            alpha = jnp.exp2(m_prev - m_new)
            p = jnp.exp2(s - jnp.tile(m_new, (1, tk // LANES)))
            l_ref[h] = alpha * l_ref[h] + jnp.sum(p, axis=1, keepdims=True)
            acc_ref[h] = alpha * acc_ref[h] + _dot(p.astype(BF16), vb_ref[pl.ds(k0, tk), cols])
            m_ref[h] = m_new

    def body(kb, carry):
        step(kb, None)
        return carry

    lax.fori_loop(0, j * diag_blocks, body, 0)
    for dblk in range(diag_blocks):
        step(j * diag_blocks + dblk, dblk)

    for h in range(H_B):
        o = acc_ref[h] / l_ref[h]
        o = o[0:tq] - lam * o[tq:2 * tq]
        y = _rms_rows(o, gsub_ref[...]) * (1.0 - lam_init)
        o_ref[:, h * LANES:(h + 1) * LANES] = y.astype(BF16)


def _diffattn(qb, kb, vb, lam_p, gsub, tq, tk, lam_init):
    b, l, w = qb.shape
    nq = l // tq
    kern = functools.partial(_diff_kernel, tq=tq, tk=tk, lam_init=lam_init)
    per_q = pl.BlockSpec((None, tq, w), lambda bi, j: (bi, j, 0))
    per_b = pl.BlockSpec((None, l, w), lambda bi, j: (bi, 0, 0))
    full = lambda a: pl.BlockSpec(a.shape, lambda bi, j: (0,) * a.ndim)
    return pl.pallas_call(
        kern, grid=(b, nq),
        in_specs=[per_q, per_b, per_b, full(lam_p), full(gsub)],
        out_specs=per_q,
        out_shape=jax.ShapeDtypeStruct((b, l, w), BF16),
        scratch_shapes=[pltpu.VMEM((H_B, 2 * tq, LANES), BF16),
                        pltpu.VMEM((H_B, 2 * tq, LANES), F32),
                        pltpu.VMEM((H_B, 2 * tq, LANES), F32),
                        pltpu.VMEM((H_B, 2 * tq, LANES), F32)],
        compiler_params=_cparams(2), name="diffattn")(qb, kb, vb, lam_p, gsub)


def _merge_kernel(h_ref, g_ref, oa_ref, ob_ref, wg_ref, wa_ref, wb_ref, wo_ref, out_ref):
    h = h_ref[...]
    d = h.shape[-1]
    u = _rms_rows(h, g_ref[...]).astype(BF16)
    gates = jax.nn.sigmoid(_dot(u, wg_ref[...]))
    mix = (gates[:, 0:d] * _dot(oa_ref[...], wa_ref[...])
           + gates[:, d:2 * d] * _dot(ob_ref[...], wb_ref[...]))
    out_ref[...] = h + _dot(mix.astype(BF16), wo_ref[...])


def _merge(h, g, oa, ob, wg, wa, wb, wo, tm):
    t, d = h.shape
    rows = lambda w: pl.BlockSpec((tm, w), lambda i: (i, 0))
    full = lambda a: pl.BlockSpec(a.shape, lambda i: (0,) * a.ndim)
    return pl.pallas_call(
        _merge_kernel, grid=(t // tm,),
        in_specs=[rows(d), full(g), rows(oa.shape[1]), rows(ob.shape[1]), full(wg), full(wa), full(wb), full(wo)],
        out_specs=rows(d), out_shape=jax.ShapeDtypeStruct((t, d), F32),
        compiler_params=_cparams(1), name="merge")(h, g, oa, ob, wg, wa, wb, wo)


def _gelu_tanh(x):
    return 0.5 * x * (1.0 + jnp.tanh(math.sqrt(2.0 / math.pi) * (x + 0.044715 * (x * x * x))))


def _ffn_kernel(h_ref, g_ref, wu_ref, cp_ref, wd_ref, out_ref, tail_ref, xa_ref, xb_ref, acc_ref,
                *, tiles_per_seq, cw):
    i = pl.program_id(0)
    tm = h_ref.shape[0]
    d_ff = wd_ref.shape[0]
    nc = d_ff // cw
    npan = cw // LANES
    h = h_ref[...]
    u = _rms_rows(h, g_ref[...]).astype(BF16)
    seq_start = (i % tiles_per_seq) == 0
    acc_ref[...] = jnp.zeros(acc_ref.shape, F32)

    def up_proj(c, x_ref):
        for half, base in enumerate((0, d_ff)):
            cols = pl.ds(pl.multiple_of(base + c * cw, LANES), cw)
            x = _dot(u, wu_ref[:, cols])
            prev = jnp.where(seq_start, 0.0, tail_ref[:, cols])
            tail_ref[:, cols] = x[tm - SUBLANES:tm, :]
            for p in range(npan):
                x_ref[half * npan + p, 0:SUBLANES, :] = prev[:, p * LANES:(p + 1) * LANES]
                x_ref[half * npan + p, SUBLANES:SUBLANES + tm, :] = x[:, p * LANES:(p + 1) * LANES]

    def gated(c, x_ref):
        outs = []
        for p in range(npan):
            conv = []
            for half, base in enumerate((0, d_ff)):
                taps = cp_ref[:, pl.ds(pl.multiple_of(base + c * cw + p * LANES, LANES), LANES)]
                q = half * npan + p
                conv.append(taps[3:4, :]
                            + x_ref[q, SUBLANES - 2:SUBLANES - 2 + tm, :] * taps[0:1, :]
                            + x_ref[q, SUBLANES - 1:SUBLANES - 1 + tm, :] * taps[1:2, :]
                            + x_ref[q, SUBLANES:SUBLANES + tm, :] * taps[2:3, :])
            outs.append((_gelu_tanh(conv[0]) * conv[1]).astype(BF16))
        return jnp.concatenate(outs, axis=1)

    def down(c, act):
        acc_ref[...] += _dot(act, wd_ref[pl.ds(pl.multiple_of(c * cw, cw), cw), :])

    up_proj(0, xa_ref)

    def pair(t, carry):
        c = 2 * t
        up_proj(c + 1, xb_ref)
        down(c, gated(c, xa_ref))
        up_proj(c + 2, xa_ref)
        down(c + 1, gated(c + 1, xb_ref))
        return carry

    lax.fori_loop(0, (nc - 1) // 2, pair, 0)
    down(nc - 1, gated(nc - 1, xa_ref))
    out_ref[...] = h + acc_ref[...]


def _convffn(h, g, wu, cp, wd, seq_len, tm, cw):
    t, d = h.shape
    rows = pl.BlockSpec((tm, d), lambda i: (i, 0))
    full = lambda a: pl.BlockSpec(a.shape, lambda i: (0,) * a.ndim)
    kern = functools.partial(_ffn_kernel, tiles_per_seq=seq_len // tm, cw=cw)
    assert (wd.shape[0] // cw) % 2 == 1, "the chunk pipeline is written for an odd chunk count"
    panels = pltpu.VMEM((2 * cw // LANES, SUBLANES + tm, LANES), F32)
    return pl.pallas_call(
        kern, grid=(t // tm,),
        in_specs=[rows, full(g), full(wu), full(cp), full(wd)],
        out_specs=rows, out_shape=jax.ShapeDtypeStruct((t, d), F32),
        scratch_shapes=[pltpu.VMEM((SUBLANES, wu.shape[1]), F32), panels, panels, pltpu.VMEM((tm, d), F32)],
        compiler_params=_cparams(1), name="convffn")(h, g, wu, cp, wd)


def _ple_kernel(h_ref, g_ref, p_ref, wg_ref, wp_ref, out_ref):
    h = h_ref[...]
    u = _rms_rows(h, g_ref[...]).astype(BF16)
    gate = jax.nn.sigmoid(_dot(u, wg_ref[...]))
    out_ref[...] = h + gate * _dot(p_ref[...].astype(BF16), wp_ref[...])


def _ple(h, g, p, wg, wp, tm):
    t, d = h.shape
    rows = lambda w: pl.BlockSpec((tm, w), lambda i: (i, 0))
    full = lambda a: pl.BlockSpec(a.shape, lambda i: (0,) * a.ndim)
    return pl.pallas_call(
        _ple_kernel, grid=(t // tm,),
        in_specs=[rows(d), full(g), rows(p.shape[1]), full(wg), full(wp)],
        out_specs=rows(d), out_shape=jax.ShapeDtypeStruct((t, d), F32),
        compiler_params=_cparams(1), name="ple")(h, g, p, wg, wp)


def _rope_tables(length):
    inv = 1.0 / (ROPE_THETA ** (jnp.arange(0, D_A, 2, dtype=F32) / D_A))
    ang = jnp.arange(length, dtype=F32)[:, None] * inv[None, :]
    c, s = jnp.cos(ang), jnp.sin(ang)
    return jnp.tile(c, (1, 4)), jnp.tile(jnp.concatenate([-s, s], axis=1), (1, 2))


def _pick_tile(n, pref):
    tile = min(n, pref)
    assert n % tile == 0, (n, tile)
    return tile


def kernel(x, p, g_mix_norm, w_in, g_qa, g_ka, g_qb, g_kb, lam_q1, lam_k1, lam_q2, lam_k2, g_subln,
           w_branch_a, w_branch_b, w_out, g_ffn_norm, w_up, conv_w, conv_b, w_down, g_ple_norm,
           w_ple_gate, w_ple_proj):
    bsz, seq_len, d_model = x.shape
    depth = w_in.shape[0]
    d_ff = w_down.shape[1]
    t = bsz * seq_len
    tq = _pick_tile(seq_len, 512)
    tk = _pick_tile(tq, 256)
    tm = _pick_tile(seq_len, 512)
    cw = 256
    assert d_ff % cw == 0

    cos_t, sin_t = _rope_tables(seq_len)
    gsum = jnp.kron(jnp.eye(8, dtype=F32), jnp.ones((64, 64), F32)).astype(BF16)
    tabs = {"cos": cos_t, "sin": sin_t, "gsum": gsum}

    sizes = [H_A * D_A, D_A, D_A, H_I * D_I, D_I, H_I, 2 * H_B * D_B, 2 * H_B * D_B, 2 * H_B * D_B, d_model, d_model]
    offs = [0]
    for sz in sizes:
        offs.append(offs[-1] + sz)
    assert offs[-1] == w_in.shape[2]

    h = x.reshape(t, d_model)
    for i in range(depth):
        lam_init = 0.8 - 0.6 * math.exp(-0.3 * i)
        segs = lax.optimization_barrier([w_in[i, :, offs[k]:offs[k + 1]] for k in range(len(sizes))])
        seg = lambda k: segs[k]
        wts = {
            "qa": seg(0).astype(BF16),
            "sm": jnp.concatenate([seg(1), seg(1), seg(2), seg(2), seg(4), seg(4),
                                   jnp.pad(seg(5), ((0, 0), (0, LANES - H_I)))], axis=1).astype(BF16),
            "qi": seg(3).astype(BF16),
            "qb": seg(6).astype(BF16), "kb": seg(7).astype(BF16), "vb": seg(8).astype(BF16),
            "gqa": jnp.tile(g_qa[i], H_A)[None, :], "gka": jnp.tile(g_ka[i], 2)[None, :],
            "gqb": jnp.tile(g_qb[i], 2 * H_B)[None, :], "gkb": jnp.tile(g_kb[i], 2 * H_B)[None, :],
        }
        qa, ka, va, qi, ki, wi_t, qb, kb, vb = _in_proj(h, g_mix_norm[i][None, :], wts, tabs, seq_len, tm)
        r3 = lambda a: a.reshape(bsz, seq_len, a.shape[-1])
        o_a = _dsa(r3(qa), r3(qi), wi_t, r3(ki), r3(ka), r3(va), tq, tk)
        lam_p = jnp.stack([lam_q1[i], lam_k1[i], lam_q2[i], lam_k2[i]])
        o_b = _diffattn(r3(qb), r3(kb), r3(vb), lam_p, g_subln[i][None, :], tq, tk, lam_init)

        w_gate = jnp.concatenate([seg(9), seg(10)], axis=1).astype(BF16)
        h = _merge(h, g_mix_norm[i][None, :], o_a.reshape(t, -1), o_b.reshape(t, -1), w_gate,
                   w_branch_a[i].astype(BF16), w_branch_b[i].astype(BF16), w_out[i].astype(BF16), tm)

        conv = jnp.concatenate([conv_w[i], conv_b[i][None, :]], axis=0)
        conv = jnp.pad(conv, ((0, SUBLANES - CONV_W - 1), (0, 0)))
        h = _convffn(h, g_ffn_norm[i][None, :], w_up[i].astype(BF16), conv, w_down[i].astype(BF16),
                     seq_len, tm, cw)

        h = _ple(h, g_ple_norm[i][None, :], p[i].reshape(t, -1), w_ple_gate[i].astype(BF16),
                 w_ple_proj[i].astype(BF16), tm)
    return h.reshape(bsz, seq_len, d_model)
```

```python
import functools
import math

import jax
import jax.numpy as jnp
from jax import lax
from jax.experimental import pallas as pl
from jax.experimental.pallas import tpu as pltpu

H_A, D_A = 8, 64
H_I, D_I = 4, 64
H_B, D_B = 4, 64
TOPK_MAX = 256
CONV_W = 3
ROPE_THETA = 10000.0
EPS = 1e-6

LANES = 128
SUBLANES = 8
PACKED_ROWS = 16
VMEM_LIMIT = 56 * 1024 * 1024

F32 = jnp.float32
BF16 = jnp.bfloat16
NEG_MASKED = -2e30
NEG_INIT = -1e30
INT_MIN = -(2 ** 31)
INT_MAX = 2 ** 31 - 1
LOG2E = math.log2(math.e)


def _nt_dot(a, b):
    return lax.dot_general(a, b, (((1,), (1,)), ((), ())), preferred_element_type=F32)


def _dot(a, b):
    return jnp.dot(a, b, preferred_element_type=F32)


def _cparams(n_axes):
    return pltpu.CompilerParams(dimension_semantics=("arbitrary",) * n_axes,
                                vmem_limit_bytes=VMEM_LIMIT)


def _rms_rows(x, g):
    ms = jnp.mean(x * x, axis=-1, keepdims=True)
    return x * lax.rsqrt(ms + EPS) * g


def _rope(x, cos_t, sin_t):
    w = x.shape[-1]
    reps = w // LANES
    c = jnp.tile(cos_t, (1, reps)) if reps > 1 else cos_t
    s = jnp.tile(sin_t, (1, reps)) if reps > 1 else sin_t
    up = pltpu.roll(x, w - 32, 1)
    dn = pltpu.roll(x, 32, 1)
    lane = lax.broadcasted_iota(jnp.int32, x.shape, 1)
    partner = jnp.where((lane & 63) < 32, up, dn)
    return x * c + partner * s


def _head_norm(x, gsum, g):
    slab = gsum.shape[0]
    sq = (x * x).astype(BF16)
    parts = [_dot(sq[:, c:c + slab], gsum) for c in range(0, x.shape[-1], slab)]
    ss = (parts[0] if len(parts) == 1 else jnp.concatenate(parts, axis=1)) * (1.0 / 64.0)
    return x * lax.rsqrt(ss + EPS) * g


def _in_proj_kernel(h_ref, g_ref, wqa_ref, wsm_ref, wqi_ref, wqb_ref, wkb_ref, wvb_ref,
                    cos_ref, sin_ref, gsum_ref, gqa_ref, gka_ref, gqb_ref, gkb_ref,
                    qa_ref, ka_ref, va_ref, qi_ref, ki_ref, wi_ref, qb_ref, kb_ref, vb_ref):
    u = _rms_rows(h_ref[...], g_ref[...]).astype(BF16)
    cos_t, sin_t = cos_ref[...], sin_ref[...]
    gsum = gsum_ref[...]

    qa = _rope(_head_norm(_dot(u, wqa_ref[...]), gsum, gqa_ref[...]), cos_t, sin_t)
    qa_ref[...] = (qa * (D_A ** -0.5 * LOG2E)).astype(BF16)

    sm = _dot(u, wsm_ref[...])
    kv, kw = sm[:, 0:LANES], sm[:, LANES:2 * LANES]
    kv_swapped, kw_swapped = pltpu.roll(kv, D_A, 1), pltpu.roll(kw, D_I, 1)
    lo_half = lax.broadcasted_iota(jnp.int32, (sm.shape[0], LANES), 1) < D_A
    ka2 = jnp.where(lo_half, kv, kv_swapped)
    ka_ref[...] = _rope(_head_norm(ka2, gsum[0:LANES, 0:LANES], gka_ref[...]), cos_t, sin_t).astype(BF16)
    va_ref[...] = jnp.where(lo_half, kv_swapped, 1.0).astype(BF16)
    ki_ref[...] = _rope(jnp.where(lo_half, kw, kw_swapped), cos_t, sin_t).astype(BF16)

    qi_ref[...] = _rope(_dot(u, wqi_ref[...]), cos_t, sin_t).astype(BF16)
    wi_t = kw.T
    wi_ref[...] = wi_t[D_I:D_I + SUBLANES, :] * (H_I ** -0.5 * D_I ** -0.5)

    qb = _rope(_head_norm(_dot(u, wqb_ref[...]), gsum, gqb_ref[...]), cos_t, sin_t)
    qb_ref[...] = (qb * (D_B ** -0.5 * LOG2E)).astype(BF16)
    kb = _rope(_head_norm(_dot(u, wkb_ref[...]), gsum, gkb_ref[...]), cos_t, sin_t)
    kb_ref[...] = kb.astype(BF16)
    vb_ref[...] = _dot(u, wvb_ref[...]).astype(BF16)


def _in_proj(h, g, w, tabs, seq_len, tm):
    t, d = h.shape
    nseq = seq_len // tm

    def full(a):
        return pl.BlockSpec(a.shape, lambda i: (0,) * a.ndim)

    def rows(width):
        return pl.BlockSpec((tm, width), lambda i: (i, 0))

    tab = pl.BlockSpec((tm, LANES), lambda i: (i % nseq, 0))
    ins = [h, g, w["qa"], w["sm"], w["qi"], w["qb"], w["kb"], w["vb"],
           tabs["cos"], tabs["sin"], tabs["gsum"], w["gqa"], w["gka"], w["gqb"], w["gkb"]]
    in_specs = [rows(d), full(g)] + [full(a) for a in ins[2:8]] + [tab, tab] + [full(a) for a in ins[10:]]
    widths = [H_A * D_A, LANES, LANES, H_I * D_I, LANES, None, 2 * H_B * D_B, 2 * H_B * D_B, 2 * H_B * D_B]
    out_shape, out_specs = [], []
    for wd in widths:
        if wd is None:
            out_shape.append(jax.ShapeDtypeStruct((SUBLANES, t), F32))
            out_specs.append(pl.BlockSpec((SUBLANES, tm), lambda i: (0, i)))
        else:
            out_shape.append(jax.ShapeDtypeStruct((t, wd), BF16))
            out_specs.append(rows(wd))
    return pl.pallas_call(
        _in_proj_kernel, grid=(t // tm,), in_specs=in_specs, out_specs=out_specs, out_shape=out_shape,
        compiler_params=_cparams(1), name="in_proj")(*ins)


def _key_to_float(k):
    return lax.bitcast_convert_type(k ^ ((k >> 31) & 0x7FFFFFFF), F32)


def _dsa_kernel(qa_ref, qi_ref, wi_ref, ki_ref, ka_ref, va_ref, ltri_ref, o_ref,
                score_ref, coarse_ref, bias_ref, qs_ref, m_ref, acc_ref,
                *, tq, tk, k_top):
    j = pl.program_id(1)
    nkb = (j + 1) * (tq // tk)
    nh = H_A
    groups = tk // SUBLANES
    groups16 = tk // PACKED_ROWS
    no_limit = 2 ** 30

    lane_q = lax.broadcasted_iota(jnp.int32, (tq, LANES), 1)
    lo_half = lane_q < 64

    qi = qi_ref[...]
    qi_heads = []
    for h in range(H_I):
        pair = qi[:, (h // 2) * LANES:(h // 2 + 1) * LANES]
        keep = lo_half if h % 2 == 0 else jnp.logical_not(lo_half)
        qi_heads.append(jnp.where(keep, pair, jnp.zeros_like(pair)))
    wi = wi_ref[...]
    t_idx = j * tq + lax.broadcasted_iota(jnp.int32, (tk, tq), 1)
    s_loc = lax.broadcasted_iota(jnp.int32, (tk, tq), 0)

    def score_body(kb, carry):
        kblk = ki_ref[pl.ds(pl.multiple_of(kb * tk, tk), tk), :]
        acc = jnp.zeros((tk, tq), F32)
        for h in range(H_I):
            acc = acc + wi[h:h + 1, :] * jnp.maximum(_nt_dot(kblk, qi_heads[h]), 0.0)
        sc = jnp.where(kb * tk + s_loc <= t_idx, acc, -jnp.inf)
        score_ref[kb] = sc.reshape(groups, SUBLANES, tq)
        coarse_ref[kb] = sc.astype(BF16).reshape(groups16, PACKED_ROWS, tq)
        return carry

    lax.fori_loop(0, nkb, score_body, 0)

    ways = 4

    def count(pred):
        def body(kb, acc):
            hit = pred(score_ref[kb]).astype(jnp.int32)
            part = groups // ways
            sums = [jnp.sum(hit[w * part:(w + 1) * part], axis=0) for w in range(ways)]
            return acc + ((sums[0] + sums[1]) + (sums[2] + sums[3]))
        acc = lax.fori_loop(0, nkb, body, jnp.zeros((SUBLANES, tq), jnp.int32))
        return jnp.sum(acc, axis=0, keepdims=True)

    def count_coarse(c16):
        one, zero = jnp.ones((), BF16), jnp.zeros((), BF16)

        def body(kb, acc):
            hit = jnp.where(coarse_ref[kb] >= c16, one, zero)
            part = groups16 // ways
            sums = [functools.reduce(lambda a, b: a + b, [hit[w * part + r] for r in range(part)])
                    for w in range(ways)]
            return acc + ((sums[0] + sums[1]) + (sums[2] + sums[3])).astype(F32)
        acc = lax.fori_loop(0, nkb, body, jnp.zeros((PACKED_ROWS, tq), F32))
        return jnp.sum(acc, axis=0, keepdims=True)

    def bcast8(v):
        return jnp.broadcast_to(v, (SUBLANES, tq))[None]

    def coarse_body(step, thr16):
        cand = jnp.where(step == 0, 0, thr16 | jnp.left_shift(1, jnp.maximum(15 - step, 0)))
        bits = jnp.left_shift(cand ^ ((cand >> 15) & 0x7FFF), 16)
        c16 = lax.bitcast_convert_type(bits, F32).astype(BF16)
        n_ge = count_coarse(jnp.broadcast_to(c16, (PACKED_ROWS, tq))[None])
        return jnp.where(n_ge >= k_top, cand, thr16)

    thr16 = lax.fori_loop(0, 16, coarse_body, jnp.full((1, tq), -(2 ** 15), jnp.int32))
    base = jnp.left_shift(thr16, 16)
    lo_key = jnp.where(base < INT_MIN + 2 ** 16, INT_MIN, base - 2 ** 16)

    def fine_body(step, thr_key):
        cand = thr_key + jnp.left_shift(1, 17 - step)
        cand = jnp.where((thr_key > 0) & (cand < 0), INT_MAX, cand)
        c8 = bcast8(_key_to_float(cand))
        n_ge = count(lambda blk: blk >= c8)
        return jnp.where(n_ge >= k_top, cand, thr_key)

    thr_key = lax.fori_loop(0, 18, fine_body, lo_key)
    n_valid = j * tq + lax.broadcasted_iota(jnp.int32, (1, tq), 1) + 1
    take_all = n_valid <= k_top
    thr = jnp.where(take_all, -jnp.inf, _key_to_float(thr_key))
    t8 = bcast8(thr)
    n_gt = count(lambda blk: blk > t8)
    need = jnp.where(take_all, no_limit, k_top - n_gt).astype(F32)

    def bias_body(kb, ties_before):
        sc = score_ref[kb].reshape(tk, tq)
        eq = sc == thr
        incl = _dot(ltri_ref[...], jnp.where(eq, 1.0, 0.0).astype(BF16))
        sel = (sc > thr) | (eq & (incl + ties_before <= need))
        sel = sel & (kb * tk + s_loc <= t_idx)
        bias_ref[kb] = jnp.where(sel, 0.0, NEG_MASKED).astype(F32).T
        return ties_before + incl[tk - 1:tk, :]

    lax.fori_loop(0, nkb, bias_body, jnp.zeros((1, tq), F32))

    qa = qa_ref[...]
    for h in range(nh):
        pair = qa[:, (h // 2) * LANES:(h // 2 + 1) * LANES]
        keep = lo_half if h % 2 == 0 else jnp.logical_not(lo_half)
        qs_ref[h * tq:(h + 1) * tq, :] = jnp.where(keep, pair, jnp.zeros_like(pair))
    m_ref[...] = jnp.full(m_ref.shape, NEG_INIT, F32)
    acc_ref[...] = jnp.zeros(acc_ref.shape, F32)

    def attn_body(kb, carry):
        k0 = pl.multiple_of(kb * tk, tk)
        kblk = ka_ref[pl.ds(k0, tk), :]
        vblk = va_ref[pl.ds(k0, tk), :]
        for h in range(nh):
            rows = slice(h * tq, (h + 1) * tq)
            s = _nt_dot(qs_ref[rows, :], kblk) + bias_ref[kb]
            m_prev = m_ref[rows, :]
            m_new = jnp.maximum(m_prev, jnp.max(s, axis=1, keepdims=True))
            p = jnp.exp2(s - jnp.tile(m_new, (1, tk // LANES)))
            acc_ref[rows, :] = acc_ref[rows, :] * jnp.exp2(m_prev - m_new) + _dot(p.astype(BF16), vblk)
            m_ref[rows, :] = m_new
        return carry

    lax.fori_loop(0, nkb, attn_body, 0)

    for pr in range(nh // 2):
        even = acc_ref[2 * pr * tq:(2 * pr + 1) * tq, :]
        odd = acc_ref[(2 * pr + 1) * tq:(2 * pr + 2) * tq, :]
        num = jnp.where(lo_half, even, pltpu.roll(odd, 64, 1))
        den = jnp.where(lo_half, pltpu.roll(even, 64, 1), odd)
        o_ref[:, pr * LANES:(pr + 1) * LANES] = (num / den).astype(BF16)


def _dsa(qa, qi, wi_t, ki, ka, va, tq, tk):
    b, l, _ = qa.shape
    nq = l // tq
    k_top = min(TOPK_MAX, l // 4)
    kern = functools.partial(_dsa_kernel, tq=tq, tk=tk, k_top=k_top)
    per_q = lambda w: pl.BlockSpec((None, tq, w), lambda bi, j: (bi, j, 0))
    per_b = pl.BlockSpec((None, l, LANES), lambda bi, j: (bi, 0, 0))
    ltri = jnp.tril(jnp.ones((tk, tk), F32)).astype(BF16)
    return pl.pallas_call(
        kern, grid=(b, nq),
        in_specs=[per_q(H_A * D_A), per_q(H_I * D_I),
                  pl.BlockSpec((SUBLANES, tq), lambda bi, j: (0, bi * nq + j)),
                  per_b, per_b, per_b,
                  pl.BlockSpec((tk, tk), lambda bi, j: (0, 0))],
        out_specs=per_q(H_A * D_A),
        out_shape=jax.ShapeDtypeStruct((b, l, H_A * D_A), BF16),
        scratch_shapes=[
            pltpu.VMEM((l // tk, tk // SUBLANES, SUBLANES, tq), F32),
            pltpu.VMEM((l // tk, tk // PACKED_ROWS, PACKED_ROWS, tq), BF16),
            pltpu.VMEM((l // tk, tq, tk), F32),
            pltpu.VMEM((H_A * tq, LANES), BF16),
            pltpu.VMEM((H_A * tq, LANES), F32),
            pltpu.VMEM((H_A * tq, LANES), F32),
        ],
        compiler_params=_cparams(2), name="dsa")(qa, qi, wi_t, ki, ka, va, ltri)


def _diff_kernel(qb_ref, kb_ref, vb_ref, lam_ref, gsub_ref, o_ref, qs_ref, m_ref, l_ref, acc_ref,
                 *, tq, tk, lam_init):
    j = pl.program_id(1)
    diag_blocks = tq // tk
    lamv = lam_ref[...]
    lam = (jnp.exp(jnp.sum(lamv[0:1] * lamv[1:2], axis=1, keepdims=True))
           - jnp.exp(jnp.sum(lamv[2:3] * lamv[3:4], axis=1, keepdims=True)) + lam_init)
    lane_q = lax.broadcasted_iota(jnp.int32, (tq, LANES), 1)
    lo_half = lane_q < 64
    row = lax.broadcasted_iota(jnp.int32, (2 * tq, tk), 0)
    t_loc = jnp.where(row >= tq, row - tq, row)
    s_loc = lax.broadcasted_iota(jnp.int32, (2 * tq, tk), 1)

    for h in range(H_B):
        pair = qb_ref[:, h * LANES:(h + 1) * LANES]
        qs_ref[h, 0:tq, :] = jnp.where(lo_half, pair, jnp.zeros_like(pair))
        qs_ref[h, tq:2 * tq, :] = jnp.where(lo_half, jnp.zeros_like(pair), pair)
    m_ref[...] = jnp.full(m_ref.shape, NEG_INIT, F32)
    l_ref[...] = jnp.zeros(l_ref.shape, F32)
    acc_ref[...] = jnp.zeros(acc_ref.shape, F32)

    def step(kb, diag):
        k0 = pl.multiple_of(kb * tk, tk)
        for h in range(H_B):
            cols = slice(h * LANES, (h + 1) * LANES)
            s = _nt_dot(qs_ref[h], kb_ref[pl.ds(k0, tk), cols])
            if diag is not None:
                s = jnp.where(diag * tk + s_loc <= t_loc, s, NEG_MASKED)
            m_prev = m_ref[h]
            m_new = jnp.maximum(m_prev, jnp.max(s, axis=1, keepdims=True))
            alpha = jnp.exp2(m_prev - m_new)
            p = jnp.exp2(s - jnp.tile(m_new, (1, tk // LANES)))
            l_ref[h] = alpha * l_ref[h] + jnp.sum(p, axis=1, keepdims=True)
            acc_ref[h] = alpha * acc_ref[h] + _dot(p.astype(BF16), vb_ref[pl.ds(k0, tk), cols])
            m_ref[h] = m_new

    def body(kb, carry):
        step(kb, None)
        return carry

    lax.fori_loop(0, j * diag_blocks, body, 0)
    for dblk in range(diag_blocks):
        step(j * diag_blocks + dblk, dblk)

    for h in range(H_B):
        o = acc_ref[h] / l_ref[h]
        o = o[0:tq] - lam * o[tq:2 * tq]
        y = _rms_rows(o, gsub_ref[...]) * (1.0 - lam_init)
        o_ref[:, h * LANES:(h + 1) * LANES] = y.astype(BF16)


def _diffattn(qb, kb, vb, lam_p, gsub, tq, tk, lam_init):
    b, l, w = qb.shape
    nq = l // tq
    kern = functools.partial(_diff_kernel, tq=tq, tk=tk, lam_init=lam_init)
    per_q = pl.BlockSpec((None, tq, w), lambda bi, j: (bi, j, 0))
    per_b = pl.BlockSpec((None, l, w), lambda bi, j: (bi, 0, 0))
    full = lambda a: pl.BlockSpec(a.shape, lambda bi, j: (0,) * a.ndim)
    return pl.pallas_call(
        kern, grid=(b, nq),
        in_specs=[per_q, per_b, per_b, full(lam_p), full(gsub)],
        out_specs=per_q,
        out_shape=jax.ShapeDtypeStruct((b, l, w), BF16),
        scratch_shapes=[pltpu.VMEM((H_B, 2 * tq, LANES), BF16),
                        pltpu.VMEM((H_B, 2 * tq, LANES), F32),
                        pltpu.VMEM((H_B, 2 * tq, LANES), F32),
                        pltpu.VMEM((H_B, 2 * tq, LANES), F32)],
        compiler_params=_cparams(2), name="diffattn")(qb, kb, vb, lam_p, gsub)


def _merge_kernel(h_ref, g_ref, oa_ref, ob_ref, wg_ref, wa_ref, wb_ref, wo_ref, out_ref):
    h = h_ref[...]
    d = h.shape[-1]
    u = _rms_rows(h, g_ref[...]).astype(BF16)
    gates = jax.nn.sigmoid(_dot(u, wg_ref[...]))
    mix = (gates[:, 0:d] * _dot(oa_ref[...], wa_ref[...])
           + gates[:, d:2 * d] * _dot(ob_ref[...], wb_ref[...]))
    out_ref[...] = h + _dot(mix.astype(BF16), wo_ref[...])


def _merge(h, g, oa, ob, wg, wa, wb, wo, tm):
    t, d = h.shape
    rows = lambda w: pl.BlockSpec((tm, w), lambda i: (i, 0))
    full = lambda a: pl.BlockSpec(a.shape, lambda i: (0,) * a.ndim)
    return pl.pallas_call(
        _merge_kernel, grid=(t // tm,),
        in_specs=[rows(d), full(g), rows(oa.shape[1]), rows(ob.shape[1]), full(wg), full(wa), full(wb), full(wo)],
        out_specs=rows(d), out_shape=jax.ShapeDtypeStruct((t, d), F32),
        compiler_params=_cparams(1), name="merge")(h, g, oa, ob, wg, wa, wb, wo)


def _gelu_tanh(x):
    return 0.5 * x * (1.0 + jnp.tanh(math.sqrt(2.0 / math.pi) * (x + 0.044715 * (x * x * x))))


def _ffn_kernel(h_ref, g_ref, wu_ref, cp_ref, wd_ref, out_ref, tail_ref, xa_ref, xb_ref, acc_ref,
                *, tiles_per_seq, cw):
    i = pl.program_id(0)
    tm = h_ref.shape[0]
    d_ff = wd_ref.shape[0]
    nc = d_ff // cw
    npan = cw // LANES
    h = h_ref[...]
    u = _rms_rows(h, g_ref[...]).astype(BF16)
    seq_start = (i % tiles_per_seq) == 0
    acc_ref[...] = jnp.zeros(acc_ref.shape, F32)

    def up_proj(c, x_ref):
        for half, base in enumerate((0, d_ff)):
            cols = pl.ds(pl.multiple_of(base + c * cw, LANES), cw)
            x = _dot(u, wu_ref[:, cols])
            prev = jnp.where(seq_start, 0.0, tail_ref[:, cols])
            tail_ref[:, cols] = x[tm - SUBLANES:tm, :]
            for p in range(npan):
                x_ref[half * npan + p, 0:SUBLANES, :] = prev[:, p * LANES:(p + 1) * LANES]
                x_ref[half * npan + p, SUBLANES:SUBLANES + tm, :] = x[:, p * LANES:(p + 1) * LANES]

    def gated(c, x_ref):
        outs = []
        for p in range(npan):
            conv = []
            for half, base in enumerate((0, d_ff)):
                taps = cp_ref[:, pl.ds(pl.multiple_of(base + c * cw + p * LANES, LANES), LANES)]
                q = half * npan + p
                conv.append(taps[3:4, :]
                            + x_ref[q, SUBLANES - 2:SUBLANES - 2 + tm, :] * taps[0:1, :]
                            + x_ref[q, SUBLANES - 1:SUBLANES - 1 + tm, :] * taps[1:2, :]
                            + x_ref[q, SUBLANES:SUBLANES + tm, :] * taps[2:3, :])
            outs.append((_gelu_tanh(conv[0]) * conv[1]).astype(BF16))
        return jnp.concatenate(outs, axis=1)

    def down(c, act):
        acc_ref[...] += _dot(act, wd_ref[pl.ds(pl.multiple_of(c * cw, cw), cw), :])

    up_proj(0, xa_ref)

    def pair(t, carry):
        c = 2 * t
        up_proj(c + 1, xb_ref)
        down(c, gated(c, xa_ref))
        up_proj(c + 2, xa_ref)
        down(c + 1, gated(c + 1, xb_ref))
        return carry

    lax.fori_loop(0, (nc - 1) // 2, pair, 0)
    down(nc - 1, gated(nc - 1, xa_ref))
    out_ref[...] = h + acc_ref[...]


def _convffn(h, g, wu, cp, wd, seq_len, tm, cw):
    t, d = h.shape
    rows = pl.BlockSpec((tm, d), lambda i: (i, 0))
    full = lambda a: pl.BlockSpec(a.shape, lambda i: (0,) * a.ndim)
    kern = functools.partial(_ffn_kernel, tiles_per_seq=seq_len // tm, cw=cw)
    assert (wd.shape[0] // cw) % 2 == 1, "the chunk pipeline is written for an odd chunk count"
    panels = pltpu.VMEM((2 * cw // LANES, SUBLANES + tm, LANES), F32)
    return pl.pallas_call(
        kern, grid=(t // tm,),
        in_specs=[rows, full(g), full(wu), full(cp), full(wd)],
        out_specs=rows, out_shape=jax.ShapeDtypeStruct((t, d), F32),
        scratch_shapes=[pltpu.VMEM((SUBLANES, wu.shape[1]), F32), panels, panels, pltpu.VMEM((tm, d), F32)],
        compiler_params=_cparams(1), name="convffn")(h, g, wu, cp, wd)


def _ple_kernel(h_ref, g_ref, p_ref, wg_ref, wp_ref, out_ref):
    h = h_ref[...]
    u = _rms_rows(h, g_ref[...]).astype(BF16)
    gate = jax.nn.sigmoid(_dot(u, wg_ref[...]))
    out_ref[...] = h + gate * _dot(p_ref[...].astype(BF16), wp_ref[...])


def _ple(h, g, p, wg, wp, tm):
    t, d = h.shape
    rows = lambda w: pl.BlockSpec((tm, w), lambda i: (i, 0))
    full = lambda a: pl.BlockSpec(a.shape, lambda i: (0,) * a.ndim)
    return pl.pallas_call(
        _ple_kernel, grid=(t // tm,),
        in_specs=[rows(d), full(g), rows(p.shape[1]), full(wg), full(wp)],
        out_specs=rows(d), out_shape=jax.ShapeDtypeStruct((t, d), F32),
        compiler_params=_cparams(1), name="ple")(h, g, p, wg, wp)


def _rope_tables(length):
    inv = 1.0 / (ROPE_THETA ** (jnp.arange(0, D_A, 2, dtype=F32) / D_A))
    ang = jnp.arange(length, dtype=F32)[:, None] * inv[None, :]
    c, s = jnp.cos(ang), jnp.sin(ang)
    return jnp.tile(c, (1, 4)), jnp.tile(jnp.concatenate([-s, s], axis=1), (1, 2))


def _pick_tile(n, pref):
    tile = min(n, pref)
    assert n % tile == 0, (n, tile)
    return tile


def kernel(x, p, g_mix_norm, w_in, g_qa, g_ka, g_qb, g_kb, lam_q1, lam_k1, lam_q2, lam_k2, g_subln,
           w_branch_a, w_branch_b, w_out, g_ffn_norm, w_up, conv_w, conv_b, w_down, g_ple_norm,
           w_ple_gate, w_ple_proj):
    bsz, seq_len, d_model = x.shape
    depth = w_in.shape[0]
    d_ff = w_down.shape[1]
    t = bsz * seq_len
    tq = _pick_tile(seq_len, 512)
    tk = _pick_tile(tq, 256)
    tm = _pick_tile(seq_len, 512)
    cw = 256
    assert d_ff % cw == 0

    cos_t, sin_t = _rope_tables(seq_len)
    gsum = jnp.kron(jnp.eye(4, dtype=F32), jnp.ones((64, 64), F32)).astype(BF16)
    tabs = {"cos": cos_t, "sin": sin_t, "gsum": gsum}

    sizes = [H_A * D_A, D_A, D_A, H_I * D_I, D_I, H_I, 2 * H_B * D_B, 2 * H_B * D_B, 2 * H_B * D_B, d_model, d_model]
    offs = [0]
    for sz in sizes:
        offs.append(offs[-1] + sz)
    assert offs[-1] == w_in.shape[2]

    h = x.reshape(t, d_model)
    for i in range(depth):
        lam_init = 0.8 - 0.6 * math.exp(-0.3 * i)
        segs = lax.optimization_barrier([w_in[i, :, offs[k]:offs[k + 1]] for k in range(len(sizes))])
        seg = lambda k: segs[k]
        wts = {
            "qa": seg(0).astype(BF16),
            "sm": jnp.concatenate([seg(1), seg(2), seg(4),
                                   jnp.pad(seg(5), ((0, 0), (0, D_I - H_I)))], axis=1).astype(BF16),
            "qi": seg(3).astype(BF16),
            "qb": seg(6).astype(BF16), "kb": seg(7).astype(BF16), "vb": seg(8).astype(BF16),
            "gqa": jnp.tile(g_qa[i], H_A)[None, :], "gka": jnp.tile(g_ka[i], 2)[None, :],
            "gqb": jnp.tile(g_qb[i], 2 * H_B)[None, :], "gkb": jnp.tile(g_kb[i], 2 * H_B)[None, :],
        }
        qa, ka, va, qi, ki, wi_t, qb, kb, vb = _in_proj(h, g_mix_norm[i][None, :], wts, tabs, seq_len, tm)
        r3 = lambda a: a.reshape(bsz, seq_len, a.shape[-1])
        o_a = _dsa(r3(qa), r3(qi), wi_t, r3(ki), r3(ka), r3(va), tq, tk)
        lam_p = jnp.stack([lam_q1[i], lam_k1[i], lam_q2[i], lam_k2[i]])
        o_b = _diffattn(r3(qb), r3(kb), r3(vb), lam_p, g_subln[i][None, :], tq, tk, lam_init)

        w_gate = jnp.concatenate([seg(9), seg(10)], axis=1).astype(BF16)
        h = _merge(h, g_mix_norm[i][None, :], o_a.reshape(t, -1), o_b.reshape(t, -1), w_gate,
                   w_branch_a[i].astype(BF16), w_branch_b[i].astype(BF16), w_out[i].astype(BF16), tm)

        conv = jnp.concatenate([conv_w[i], conv_b[i][None, :]], axis=0)
        conv = jnp.pad(conv, ((0, SUBLANES - CONV_W - 1), (0, 0)))
        h = _convffn(h, g_ffn_norm[i][None, :], w_up[i].astype(BF16), conv, w_down[i].astype(BF16),
                     seq_len, tm, cw)

        h = _ple(h, g_ple_norm[i][None, :], p[i].reshape(t, -1), w_ple_gate[i].astype(BF16),
                 w_ple_proj[i].astype(BF16), tm)
    return h.reshape(bsz, seq_len, d_model)
```

```python
import functools
import math

import jax
import jax.numpy as jnp
from jax import lax
from jax.experimental import pallas as pl
from jax.experimental.pallas import tpu as pltpu

H_A, D_A = 8, 64
H_I, D_I = 4, 64
H_B, D_B = 4, 64
TOPK_MAX = 256
CONV_W = 3
ROPE_THETA = 10000.0
EPS = 1e-6

LANES = 128
SUBLANES = 8
PACKED_ROWS = 16
VMEM_LIMIT = 56 * 1024 * 1024

F32 = jnp.float32
BF16 = jnp.bfloat16
NEG_MASKED = -2e30
NEG_INIT = -1e30
INT_MIN = -(2 ** 31)
INT_MAX = 2 ** 31 - 1
LOG2E = math.log2(math.e)


def _nt_dot(a, b):
    return lax.dot_general(a, b, (((1,), (1,)), ((), ())), preferred_element_type=F32)


def _dot(a, b):
    return jnp.dot(a, b, preferred_element_type=F32)


def _cparams(n_axes):
    return pltpu.CompilerParams(dimension_semantics=("arbitrary",) * n_axes,
                                vmem_limit_bytes=VMEM_LIMIT)


def _rms_rows(x, g):
    ms = jnp.mean(x * x, axis=-1, keepdims=True)
    return x * lax.rsqrt(ms + EPS) * g


def _rope(x, cos_t, sin_t):
    w = x.shape[-1]
    reps = w // LANES
    c = jnp.tile(cos_t, (1, reps)) if reps > 1 else cos_t
    s = jnp.tile(sin_t, (1, reps)) if reps > 1 else sin_t
    up = pltpu.roll(x, w - 32, 1)
    dn = pltpu.roll(x, 32, 1)
    lane = lax.broadcasted_iota(jnp.int32, x.shape, 1)
    partner = jnp.where((lane & 63) < 32, up, dn)
    return x * c + partner * s


def _head_norm(x, gsum, g):
    slab = gsum.shape[0]
    sq = (x * x).astype(BF16)
    parts = [_dot(sq[:, c:c + slab], gsum) for c in range(0, x.shape[-1], slab)]
    ss = (parts[0] if len(parts) == 1 else jnp.concatenate(parts, axis=1)) * (1.0 / 64.0)
    return x * lax.rsqrt(ss + EPS) * g


def _in_proj_kernel(h_ref, g_ref, wqa_ref, wsm_ref, wqi_ref, wqb_ref, wkb_ref, wvb_ref,
                    cos_ref, sin_ref, gsum_ref, gqa_ref, gka_ref, gqb_ref, gkb_ref,
                    qa_ref, ka_ref, va_ref, qi_ref, ki_ref, wi_ref, qb_ref, kb_ref, vb_ref):
    u = _rms_rows(h_ref[...], g_ref[...]).astype(BF16)
    cos_t, sin_t = cos_ref[...], sin_ref[...]
    gsum = gsum_ref[...]

    qa = _rope(_head_norm(_dot(u, wqa_ref[...]), gsum, gqa_ref[...]), cos_t, sin_t)
    qa_ref[...] = (qa * (D_A ** -0.5 * LOG2E)).astype(BF16)

    sm = _dot(u, wsm_ref[...])
    kv, kw = sm[:, 0:LANES], sm[:, LANES:2 * LANES]
    kv_swapped, kw_swapped = pltpu.roll(kv, D_A, 1), pltpu.roll(kw, D_I, 1)
    lo_half = lax.broadcasted_iota(jnp.int32, (sm.shape[0], LANES), 1) < D_A
    ka2 = jnp.where(lo_half, kv, kv_swapped)
    ka_ref[...] = _rope(_head_norm(ka2, gsum[0:LANES, 0:LANES], gka_ref[...]), cos_t, sin_t).astype(BF16)
    va_ref[...] = jnp.where(lo_half, kv_swapped, 1.0).astype(BF16)
    ki_ref[...] = _rope(jnp.where(lo_half, kw, kw_swapped), cos_t, sin_t).astype(BF16)

    qi_ref[...] = _rope(_dot(u, wqi_ref[...]), cos_t, sin_t).astype(BF16)
    wi_t = kw.T
    wi_ref[...] = wi_t[D_I:D_I + SUBLANES, :] * (H_I ** -0.5 * D_I ** -0.5)

    qb = _rope(_head_norm(_dot(u, wqb_ref[...]), gsum, gqb_ref[...]), cos_t, sin_t)
    qb_ref[...] = (qb * (D_B ** -0.5 * LOG2E)).astype(BF16)
    kb = _rope(_head_norm(_dot(u, wkb_ref[...]), gsum, gkb_ref[...]), cos_t, sin_t)
    kb_ref[...] = kb.astype(BF16)
    vb_ref[...] = _dot(u, wvb_ref[...]).astype(BF16)


def _in_proj(h, g, w, tabs, seq_len, tm):
    t, d = h.shape
    nseq = seq_len // tm

    def full(a):
        return pl.BlockSpec(a.shape, lambda i: (0,) * a.ndim)

    def rows(width):
        return pl.BlockSpec((tm, width), lambda i: (i, 0))

    tab = pl.BlockSpec((tm, LANES), lambda i: (i % nseq, 0))
    ins = [h, g, w["qa"], w["sm"], w["qi"], w["qb"], w["kb"], w["vb"],
           tabs["cos"], tabs["sin"], tabs["gsum"], w["gqa"], w["gka"], w["gqb"], w["gkb"]]
    in_specs = [rows(d), full(g)] + [full(a) for a in ins[2:8]] + [tab, tab] + [full(a) for a in ins[10:]]
    widths = [H_A * D_A, LANES, LANES, H_I * D_I, LANES, None, 2 * H_B * D_B, 2 * H_B * D_B, 2 * H_B * D_B]
    out_shape, out_specs = [], []
    for wd in widths:
        if wd is None:
            out_shape.append(jax.ShapeDtypeStruct((SUBLANES, t), F32))
            out_specs.append(pl.BlockSpec((SUBLANES, tm), lambda i: (0, i)))
        else:
            out_shape.append(jax.ShapeDtypeStruct((t, wd), BF16))
            out_specs.append(rows(wd))
    return pl.pallas_call(
        _in_proj_kernel, grid=(t // tm,), in_specs=in_specs, out_specs=out_specs, out_shape=out_shape,
        compiler_params=_cparams(1), name="in_proj")(*ins)


def _key_to_float(k):
    return lax.bitcast_convert_type(k ^ ((k >> 31) & 0x7FFFFFFF), F32)


def _dsa_kernel(qa_ref, qi_ref, wi_ref, ki_ref, ka_ref, va_ref, ltri_ref, o_ref,
                score_ref, coarse_ref, bias_ref, qs_ref, m_ref, acc_ref,
                *, tq, tk, k_top):
    j = pl.program_id(1)
    nkb = (j + 1) * (tq // tk)
    nh = H_A
    groups = tk // SUBLANES
    groups16 = tk // PACKED_ROWS
    no_limit = 2 ** 30

    lane_q = lax.broadcasted_iota(jnp.int32, (tq, LANES), 1)
    lo_half = lane_q < 64

    qi = qi_ref[...]
    qi_heads = []
    for h in range(H_I):
        pair = qi[:, (h // 2) * LANES:(h // 2 + 1) * LANES]
        keep = lo_half if h % 2 == 0 else jnp.logical_not(lo_half)
        qi_heads.append(jnp.where(keep, pair, jnp.zeros_like(pair)))
    wi = wi_ref[...]
    t_idx = j * tq + lax.broadcasted_iota(jnp.int32, (tk, tq), 1)
    s_loc = lax.broadcasted_iota(jnp.int32, (tk, tq), 0)

    def score_body(kb, carry):
        kblk = ki_ref[pl.ds(pl.multiple_of(kb * tk, tk), tk), :]
        acc = jnp.zeros((tk, tq), F32)
        for h in range(H_I):
            acc = acc + wi[h:h + 1, :] * jnp.maximum(_nt_dot(kblk, qi_heads[h]), 0.0)
        sc = jnp.where(kb * tk + s_loc <= t_idx, acc, -jnp.inf)
        score_ref[kb] = sc.reshape(groups, SUBLANES, tq)
        coarse_ref[kb] = sc.astype(BF16).reshape(groups16, PACKED_ROWS, tq)
        return carry

    lax.fori_loop(0, nkb, score_body, 0)

    ways = 4

    def count(pred):
        def body(kb, acc):
            hit = pred(score_ref[kb]).astype(jnp.int32)
            part = groups // ways
            sums = [jnp.sum(hit[w * part:(w + 1) * part], axis=0) for w in range(ways)]
            return acc + ((sums[0] + sums[1]) + (sums[2] + sums[3]))
        acc = lax.fori_loop(0, nkb, body, jnp.zeros((SUBLANES, tq), jnp.int32))
        return jnp.sum(acc, axis=0, keepdims=True)

    def count_coarse(c16):
        one, zero = jnp.ones((), BF16), jnp.zeros((), BF16)

        def body(kb, acc):
            hit = jnp.where(coarse_ref[kb] >= c16, one, zero)
            part = groups16 // ways
            sums = [functools.reduce(lambda a, b: a + b, [hit[w * part + r] for r in range(part)])
                    for w in range(ways)]
            return acc + ((sums[0] + sums[1]) + (sums[2] + sums[3])).astype(F32)
        acc = lax.fori_loop(0, nkb, body, jnp.zeros((PACKED_ROWS, tq), F32))
        return jnp.sum(acc, axis=0, keepdims=True)

    def bcast8(v):
        return jnp.broadcast_to(v, (SUBLANES, tq))[None]

    def coarse_body(step, thr16):
        cand = jnp.where(step == 0, 0, thr16 | jnp.left_shift(1, jnp.maximum(15 - step, 0)))
        bits = jnp.left_shift(cand ^ ((cand >> 15) & 0x7FFF), 16)
        c16 = lax.bitcast_convert_type(bits, F32).astype(BF16)
        n_ge = count_coarse(jnp.broadcast_to(c16, (PACKED_ROWS, tq))[None])
        return jnp.where(n_ge >= k_top, cand, thr16)

    thr16 = lax.fori_loop(0, 16, coarse_body, jnp.full((1, tq), -(2 ** 15), jnp.int32))
    base = jnp.left_shift(thr16, 16)
    lo_key = jnp.where(base < INT_MIN + 2 ** 16, INT_MIN, base - 2 ** 16)

    def fine_body(step, thr_key):
        cand = thr_key + jnp.left_shift(1, 17 - step)
        cand = jnp.where((thr_key > 0) & (cand < 0), INT_MAX, cand)
        c8 = bcast8(_key_to_float(cand))
        n_ge = count(lambda blk: blk >= c8)
        return jnp.where(n_ge >= k_top, cand, thr_key)

    thr_key = lax.fori_loop(0, 18, fine_body, lo_key)
    n_valid = j * tq + lax.broadcasted_iota(jnp.int32, (1, tq), 1) + 1
    take_all = n_valid <= k_top
    thr = jnp.where(take_all, -jnp.inf, _key_to_float(thr_key))
    t8 = bcast8(thr)
    n_gt = count(lambda blk: blk > t8)
    need = jnp.where(take_all, no_limit, k_top - n_gt).astype(F32)

    def bias_body(kb, ties_before):
        sc = score_ref[kb].reshape(tk, tq)
        eq = sc == thr
        incl = _dot(ltri_ref[...], jnp.where(eq, 1.0, 0.0).astype(BF16))
        sel = (sc > thr) | (eq & (incl + ties_before <= need))
        sel = sel & (kb * tk + s_loc <= t_idx)
        bias_ref[kb] = jnp.where(sel, 0.0, NEG_MASKED).astype(F32).T
        return ties_before + incl[tk - 1:tk, :]

    lax.fori_loop(0, nkb, bias_body, jnp.zeros((1, tq), F32))

    qa = qa_ref[...]
    for h in range(nh):
        pair = qa[:, (h // 2) * LANES:(h // 2 + 1) * LANES]
        keep = lo_half if h % 2 == 0 else jnp.logical_not(lo_half)
        qs_ref[h * tq:(h + 1) * tq, :] = jnp.where(keep, pair, jnp.zeros_like(pair))
    m_ref[...] = jnp.full(m_ref.shape, NEG_INIT, F32)
    acc_ref[...] = jnp.zeros(acc_ref.shape, F32)

    def attn_body(kb, carry):
        k0 = pl.multiple_of(kb * tk, tk)
        kblk = ka_ref[pl.ds(k0, tk), :]
        vblk = va_ref[pl.ds(k0, tk), :]
        for h in range(nh):
            rows = slice(h * tq, (h + 1) * tq)
            s = _nt_dot(qs_ref[rows, :], kblk) + bias_ref[kb]
            m_prev = m_ref[rows, :]
            m_new = jnp.maximum(m_prev, jnp.max(s, axis=1, keepdims=True))
            p = jnp.exp2(s - jnp.tile(m_new, (1, tk // LANES)))
            acc_ref[rows, :] = acc_ref[rows, :] * jnp.exp2(m_prev - m_new) + _dot(p.astype(BF16), vblk)
            m_ref[rows, :] = m_new
        return carry

    lax.fori_loop(0, nkb, attn_body, 0)

    for pr in range(nh // 2):
        even = acc_ref[2 * pr * tq:(2 * pr + 1) * tq, :]
        odd = acc_ref[(2 * pr + 1) * tq:(2 * pr + 2) * tq, :]
        num = jnp.where(lo_half, even, pltpu.roll(odd, 64, 1))
        den = jnp.where(lo_half, pltpu.roll(even, 64, 1), odd)
        o_ref[:, pr * LANES:(pr + 1) * LANES] = (num / den).astype(BF16)


def _dsa(qa, qi, wi_t, ki, ka, va, tq, tk):
    b, l, _ = qa.shape
    nq = l // tq
    k_top = min(TOPK_MAX, l // 4)
    kern = functools.partial(_dsa_kernel, tq=tq, tk=tk, k_top=k_top)
    per_q = lambda w: pl.BlockSpec((None, tq, w), lambda bi, j: (bi, j, 0))
    per_b = pl.BlockSpec((None, l, LANES), lambda bi, j: (bi, 0, 0))
    ltri = jnp.tril(jnp.ones((tk, tk), F32)).astype(BF16)
    return pl.pallas_call(
        kern, grid=(b, nq),
        in_specs=[per_q(H_A * D_A), per_q(H_I * D_I),
                  pl.BlockSpec((SUBLANES, tq), lambda bi, j: (0, bi * nq + j)),
                  per_b, per_b, per_b,
                  pl.BlockSpec((tk, tk), lambda bi, j: (0, 0))],
        out_specs=per_q(H_A * D_A),
        out_shape=jax.ShapeDtypeStruct((b, l, H_A * D_A), BF16),
        scratch_shapes=[
            pltpu.VMEM((l // tk, tk // SUBLANES, SUBLANES, tq), F32),
            pltpu.VMEM((l // tk, tk // PACKED_ROWS, PACKED_ROWS, tq), BF16),
            pltpu.VMEM((l // tk, tq, tk), F32),
            pltpu.VMEM((H_A * tq, LANES), BF16),
            pltpu.VMEM((H_A * tq, LANES), F32),
            pltpu.VMEM((H_A * tq, LANES), F32),
        ],
        compiler_params=_cparams(2), name="dsa")(qa, qi, wi_t, ki, ka, va, ltri)


def _diff_kernel(qb_ref, kb_ref, vb_ref, lam_ref, gsub_ref, o_ref, qs_ref, m_ref, l_ref, acc_ref,
                 *, tq, tk, lam_init):
    j = pl.program_id(1)
    diag_blocks = tq // tk
    lamv = lam_ref[...]
    lam = (jnp.exp(jnp.sum(lamv[0:1] * lamv[1:2], axis=1, keepdims=True))
           - jnp.exp(jnp.sum(lamv[2:3] * lamv[3:4], axis=1, keepdims=True)) + lam_init)
    lane_q = lax.broadcasted_iota(jnp.int32, (tq, LANES), 1)
    lo_half = lane_q < 64
    row = lax.broadcasted_iota(jnp.int32, (2 * tq, tk), 0)
    t_loc = jnp.where(row >= tq, row - tq, row)
    s_loc = lax.broadcasted_iota(jnp.int32, (2 * tq, tk), 1)

    for h in range(H_B):
        pair = qb_ref[:, h * LANES:(h + 1) * LANES]
        qs_ref[h, 0:tq, :] = jnp.where(lo_half, pair, jnp.zeros_like(pair))
        qs_ref[h, tq:2 * tq, :] = jnp.where(lo_half, jnp.zeros_like(pair), pair)
    m_ref[...] = jnp.full(m_ref.shape, NEG_INIT, F32)
    l_ref[...] = jnp.zeros(l_ref.shape, F32)
    acc_ref[...] = jnp.zeros(acc_ref.shape, F32)

    def step(kb, diag):
        k0 = pl.multiple_of(kb * tk, tk)
        for h in range(H_B):
            cols = slice(h * LANES, (h + 1) * LANES)
            s = _nt_dot(qs_ref[h], kb_ref[pl.ds(k0, tk), cols])
            if diag is not None:
                s = jnp.where(diag * tk + s_loc <= t_loc, s, NEG_MASKED)
            m_prev = m_ref[h]
            m_new = jnp.maximum(m_prev, jnp.max(s, axis=1, keepdims=True))
            alpha = jnp.exp2(m_prev - m_new)
            p = jnp.exp2(s - jnp.tile(m_new, (1, tk // LANES)))
            l_ref[h] = alpha * l_ref[h] + jnp.sum(p, axis=1, keepdims=True)
            acc_ref[h] = alpha * acc_ref[h] + _dot(p.astype(BF16), vb_ref[pl.ds(k0, tk), cols])
            m_ref[h] = m_new

    def body(kb, carry):
        step(kb, None)
        return carry

    lax.fori_loop(0, j * diag_blocks, body, 0)
    for dblk in range(diag_blocks):
        step(j * diag_blocks + dblk, dblk)

    for h in range(H_B):
        o = acc_ref[h] / l_ref[h]
        o = o[0:tq] - lam * o[tq:2 * tq]
        y = _rms_rows(o, gsub_ref[...]) * (1.0 - lam_init)
        o_ref[:, h * LANES:(h + 1) * LANES] = y.astype(BF16)


def _diffattn(qb, kb, vb, lam_p, gsub, tq, tk, lam_init):
    b, l, w = qb.shape
    nq = l // tq
    kern = functools.partial(_diff_kernel, tq=tq, tk=tk, lam_init=lam_init)
    per_q = pl.BlockSpec((None, tq, w), lambda bi, j: (bi, j, 0))
    per_b = pl.BlockSpec((None, l, w), lambda bi, j: (bi, 0, 0))
    full = lambda a: pl.BlockSpec(a.shape, lambda bi, j: (0,) * a.ndim)
    return pl.pallas_call(
        kern, grid=(b, nq),
        in_specs=[per_q, per_b, per_b, full(lam_p), full(gsub)],
        out_specs=per_q,
        out_shape=jax.ShapeDtypeStruct((b, l, w), BF16),
        scratch_shapes=[pltpu.VMEM((H_B, 2 * tq, LANES), BF16),
                        pltpu.VMEM((H_B, 2 * tq, LANES), F32),
                        pltpu.VMEM((H_B, 2 * tq, LANES), F32),
                        pltpu.VMEM((H_B, 2 * tq, LANES), F32)],
        compiler_params=_cparams(2), name="diffattn")(qb, kb, vb, lam_p, gsub)


def _merge_kernel(h_ref, g_ref, oa_ref, ob_ref, wg_ref, wa_ref, wb_ref, wo_ref, out_ref):
    h = h_ref[...]
    d = h.shape[-1]
    u = _rms_rows(h, g_ref[...]).astype(BF16)
    gates = jax.nn.sigmoid(_dot(u, wg_ref[...]))
    mix = (gates[:, 0:d] * _dot(oa_ref[...], wa_ref[...])
           + gates[:, d:2 * d] * _dot(ob_ref[...], wb_ref[...]))
    out_ref[...] = h + _dot(mix.astype(BF16), wo_ref[...])


def _merge(h, g, oa, ob, wg, wa, wb, wo, tm):
    t, d = h.shape
    rows = lambda w: pl.BlockSpec((tm, w), lambda i: (i, 0))
    full = lambda a: pl.BlockSpec(a.shape, lambda i: (0,) * a.ndim)
    return pl.pallas_call(
        _merge_kernel, grid=(t // tm,),
        in_specs=[rows(d), full(g), rows(oa.shape[1]), rows(ob.shape[1]), full(wg), full(wa), full(wb), full(wo)],
        out_specs=rows(d), out_shape=jax.ShapeDtypeStruct((t, d), F32),
        compiler_params=_cparams(1), name="merge")(h, g, oa, ob, wg, wa, wb, wo)


def _gelu_tanh(x):
    return 0.5 * x * (1.0 + jnp.tanh(math.sqrt(2.0 / math.pi) * (x + 0.044715 * (x * x * x))))


def _ffn_ple_kernel(h_ref, hn_ref, g_ref, wu_ref, cp_ref, wd_ref, p_ref, gp_ref, wpg_ref, wpp_ref, out_ref,
                    tail_ref, xa_ref, xb_ref, xn_ref, acc_ref, *, tiles_per_seq, cw):
    i = pl.program_id(0)
    tm = h_ref.shape[0]
    d_ff = wd_ref.shape[0]
    nc = d_ff // cw
    npan = cw // LANES
    h = h_ref[...]
    acc_ref[...] = jnp.zeros(acc_ref.shape, F32)

    def up_proj(u, c, x_ref, seq_start):
        for half, base in enumerate((0, d_ff)):
            cols = pl.ds(pl.multiple_of(base + c * cw, LANES), cw)
            x = _dot(u, wu_ref[:, cols])
            prev = jnp.where(seq_start, 0.0, tail_ref[:, cols])
            tail_ref[:, cols] = x[tm - SUBLANES:tm, :]
            for p in range(npan):
                x_ref[half * npan + p, 0:SUBLANES, :] = prev[:, p * LANES:(p + 1) * LANES]
                x_ref[half * npan + p, SUBLANES:SUBLANES + tm, :] = x[:, p * LANES:(p + 1) * LANES]

    def gated(c, x_ref):
        outs = []
        for p in range(npan):
            conv = []
            for half, base in enumerate((0, d_ff)):
                taps = cp_ref[:, pl.ds(pl.multiple_of(base + c * cw + p * LANES, LANES), LANES)]
                q = half * npan + p
                conv.append(taps[3:4, :]
                            + x_ref[q, SUBLANES - 2:SUBLANES - 2 + tm, :] * taps[0:1, :]
                            + x_ref[q, SUBLANES - 1:SUBLANES - 1 + tm, :] * taps[1:2, :]
                            + x_ref[q, SUBLANES:SUBLANES + tm, :] * taps[2:3, :])
            outs.append((_gelu_tanh(conv[0]) * conv[1]).astype(BF16))
        return jnp.concatenate(outs, axis=1)

    def down(c, act):
        acc_ref[...] += _dot(act, wd_ref[pl.ds(pl.multiple_of(c * cw, cw), cw), :])

    @pl.when(i == 0)
    def _():
        up_proj(_rms_rows(h, g_ref[...]).astype(BF16), 0, xn_ref, True)

    u = _rms_rows(h, g_ref[...]).astype(BF16)
    seq_start = (i % tiles_per_seq) == 0
    up_proj(u, 1, xa_ref, seq_start)
    down(0, gated(0, xn_ref))
    up_proj(u, 2, xb_ref, seq_start)
    down(1, gated(1, xa_ref))

    def pair(t, carry):
        c = 2 * t + 2
        up_proj(u, c + 1, xa_ref, seq_start)
        down(c, gated(c, xb_ref))
        up_proj(u, c + 2, xb_ref, seq_start)
        down(c + 1, gated(c + 1, xa_ref))
        return carry

    lax.fori_loop(0, (nc - 3) // 2, pair, 0)
    up_proj(_rms_rows(hn_ref[...], g_ref[...]).astype(BF16), 0, xn_ref, ((i + 1) % tiles_per_seq) == 0)
    down(nc - 1, gated(nc - 1, xb_ref))
    h2 = h + acc_ref[...]
    gate = jax.nn.sigmoid(_dot(_rms_rows(h2, gp_ref[...]).astype(BF16), wpg_ref[...]))
    out_ref[...] = h2 + gate * _dot(p_ref[...].astype(BF16), wpp_ref[...])


def _convffn_ple(h, g, wu, cp, wd, p, gp, wpg, wpp, seq_len, tm, cw):
    t, d = h.shape
    n = t // tm
    nc = wd.shape[0] // cw
    assert nc % 2 == 1 and nc >= 3, "the chunk pipeline is written for an odd chunk count"
    rows = lambda w: pl.BlockSpec((tm, w), lambda i: (i, 0))
    next_rows = pl.BlockSpec((tm, d), lambda i: (jnp.minimum(i + 1, n - 1), 0))
    const = lambda a: pl.BlockSpec(a.shape, lambda i: (0,) * a.ndim, pipeline_mode=pl.Buffered(1))
    kern = functools.partial(_ffn_ple_kernel, tiles_per_seq=seq_len // tm, cw=cw)
    panels = pltpu.VMEM((2 * cw // LANES, SUBLANES + tm, LANES), F32)
    return pl.pallas_call(
        kern, grid=(n,),
        in_specs=[rows(d), next_rows, const(g), const(wu), const(cp), const(wd),
                  rows(p.shape[1]), const(gp), const(wpg), const(wpp)],
        out_specs=rows(d), out_shape=jax.ShapeDtypeStruct((t, d), F32),
        scratch_shapes=[pltpu.VMEM((SUBLANES, wu.shape[1]), F32), panels, panels, panels,
                        pltpu.VMEM((tm, d), F32)],
        compiler_params=_cparams(1), name="convffn_ple")(h, h, g, wu, cp, wd, p, gp, wpg, wpp)


def _rope_tables(length):
    inv = 1.0 / (ROPE_THETA ** (jnp.arange(0, D_A, 2, dtype=F32) / D_A))
    ang = jnp.arange(length, dtype=F32)[:, None] * inv[None, :]
    c, s = jnp.cos(ang), jnp.sin(ang)
    return jnp.tile(c, (1, 4)), jnp.tile(jnp.concatenate([-s, s], axis=1), (1, 2))


def _pick_tile(n, pref):
    tile = min(n, pref)
    assert n % tile == 0, (n, tile)
    return tile


def kernel(x, p, g_mix_norm, w_in, g_qa, g_ka, g_qb, g_kb, lam_q1, lam_k1, lam_q2, lam_k2, g_subln,
           w_branch_a, w_branch_b, w_out, g_ffn_norm, w_up, conv_w, conv_b, w_down, g_ple_norm,
           w_ple_gate, w_ple_proj):
    bsz, seq_len, d_model = x.shape
    depth = w_in.shape[0]
    d_ff = w_down.shape[1]
    t = bsz * seq_len
    tq = _pick_tile(seq_len, 512)
    tk = _pick_tile(tq, 256)
    tk_diff = _pick_tile(tq, 512)
    tm = _pick_tile(seq_len, 512)
    cw = 256
    assert d_ff % cw == 0

    cos_t, sin_t = _rope_tables(seq_len)
    gsum = jnp.kron(jnp.eye(4, dtype=F32), jnp.ones((64, 64), F32)).astype(BF16)
    tabs = {"cos": cos_t, "sin": sin_t, "gsum": gsum}

    sizes = [H_A * D_A, D_A, D_A, H_I * D_I, D_I, H_I, 2 * H_B * D_B, 2 * H_B * D_B, 2 * H_B * D_B, d_model, d_model]
    offs = [0]
    for sz in sizes:
        offs.append(offs[-1] + sz)
    assert offs[-1] == w_in.shape[2]

    h = x.reshape(t, d_model)
    for i in range(depth):
        lam_init = 0.8 - 0.6 * math.exp(-0.3 * i)
        segs = lax.optimization_barrier([w_in[i, :, offs[k]:offs[k + 1]] for k in range(len(sizes))])
        seg = lambda k: segs[k]
        wts = {
            "qa": seg(0).astype(BF16),
            "sm": jnp.concatenate([seg(1), seg(2), seg(4),
                                   jnp.pad(seg(5), ((0, 0), (0, D_I - H_I)))], axis=1).astype(BF16),
            "qi": seg(3).astype(BF16),
            "qb": seg(6).astype(BF16), "kb": seg(7).astype(BF16), "vb": seg(8).astype(BF16),
            "gqa": jnp.tile(g_qa[i], H_A)[None, :], "gka": jnp.tile(g_ka[i], 2)[None, :],
            "gqb": jnp.tile(g_qb[i], 2 * H_B)[None, :], "gkb": jnp.tile(g_kb[i], 2 * H_B)[None, :],
        }
        qa, ka, va, qi, ki, wi_t, qb, kb, vb = _in_proj(h, g_mix_norm[i][None, :], wts, tabs, seq_len, tm)
        r3 = lambda a: a.reshape(bsz, seq_len, a.shape[-1])
        o_a = _dsa(r3(qa), r3(qi), wi_t, r3(ki), r3(ka), r3(va), tq, tk)
        lam_p = jnp.stack([lam_q1[i], lam_k1[i], lam_q2[i], lam_k2[i]])
        o_b = _diffattn(r3(qb), r3(kb), r3(vb), lam_p, g_subln[i][None, :], tq, tk_diff, lam_init)

        w_gate = jnp.concatenate([seg(9), seg(10)], axis=1).astype(BF16)
        h = _merge(h, g_mix_norm[i][None, :], o_a.reshape(t, -1), o_b.reshape(t, -1), w_gate,
                   w_branch_a[i].astype(BF16), w_branch_b[i].astype(BF16), w_out[i].astype(BF16), tm)

        conv = jnp.concatenate([conv_w[i], conv_b[i][None, :]], axis=0)
        conv = jnp.pad(conv, ((0, SUBLANES - CONV_W - 1), (0, 0)))
        h = _convffn_ple(h, g_ffn_norm[i][None, :], w_up[i].astype(BF16), conv, w_down[i].astype(BF16),
                         p[i].reshape(t, -1), g_ple_norm[i][None, :], w_ple_gate[i].astype(BF16),
                         w_ple_proj[i].astype(BF16), seq_len, tm, cw)
    return h.reshape(bsz, seq_len, d_model)
```

```python
import functools
import math
import operator

import jax
import jax.numpy as jnp
from jax import lax
from jax.experimental import pallas as pl
from jax.experimental.pallas import tpu as pltpu

H_A, D_A = 8, 64
H_I, D_I = 4, 64
H_B, D_B = 4, 64
TOPK_MAX = 256
CONV_W = 3
ROPE_THETA = 10000.0
EPS = 1e-6

LANES = 128
SUBLANES = 8
PACKED_ROWS = 16
VMEM_LIMIT = 56 * 1024 * 1024

F32 = jnp.float32
BF16 = jnp.bfloat16
NEG_MASKED = -2e30
NEG_INIT = -1e30
INT_MIN = -(2 ** 31)
INT_MAX = 2 ** 31 - 1
LOG2E = math.log2(math.e)


def _nt_dot(a, b):
    return lax.dot_general(a, b, (((1,), (1,)), ((), ())), preferred_element_type=F32)


def _dot(a, b):
    return jnp.dot(a, b, preferred_element_type=F32)


def _cparams(n_axes):
    return pltpu.CompilerParams(dimension_semantics=("arbitrary",) * n_axes,
                                vmem_limit_bytes=VMEM_LIMIT)


def _rms_rows(x, g):
    ms = jnp.mean(x * x, axis=-1, keepdims=True)
    return x * lax.rsqrt(ms + EPS) * g


def _rope(x, cos_t, sin_t):
    w = x.shape[-1]
    reps = w // LANES
    c = jnp.tile(cos_t, (1, reps)) if reps > 1 else cos_t
    s = jnp.tile(sin_t, (1, reps)) if reps > 1 else sin_t
    up = pltpu.roll(x, w - 32, 1)
    dn = pltpu.roll(x, 32, 1)
    lane = lax.broadcasted_iota(jnp.int32, x.shape, 1)
    partner = jnp.where((lane & 63) < 32, up, dn)
    return x * c + partner * s


def _head_norm(x, gsum, g):
    slab = gsum.shape[0]
    sq = (x * x).astype(BF16)
    parts = [_dot(sq[:, c:c + slab], gsum) for c in range(0, x.shape[-1], slab)]
    ss = (parts[0] if len(parts) == 1 else jnp.concatenate(parts, axis=1)) * (1.0 / 64.0)
    return x * lax.rsqrt(ss + EPS) * g


def _in_proj_kernel(h_ref, g_ref, wqa_ref, wsm_ref, wqi_ref, wqb_ref, wkb_ref, wvb_ref,
                    cos_ref, sin_ref, gsum_ref, gqa_ref, gka_ref, gqb_ref, gkb_ref,
                    qa_ref, ka_ref, va_ref, qi_ref, ki_ref, wi_ref, qb_ref, kb_ref, vb_ref):
    u = _rms_rows(h_ref[...], g_ref[...]).astype(BF16)
    cos_t, sin_t = cos_ref[...], sin_ref[...]
    gsum = gsum_ref[...]

    qa = _rope(_head_norm(_dot(u, wqa_ref[...]), gsum, gqa_ref[...]), cos_t, sin_t)
    qa_ref[...] = (qa * (D_A ** -0.5 * LOG2E)).astype(BF16)

    sm = _dot(u, wsm_ref[...])
    kv, kw = sm[:, 0:LANES], sm[:, LANES:2 * LANES]
    kv_swapped, kw_swapped = pltpu.roll(kv, D_A, 1), pltpu.roll(kw, D_I, 1)
    lo_half = lax.broadcasted_iota(jnp.int32, (sm.shape[0], LANES), 1) < D_A
    ka2 = jnp.where(lo_half, kv, kv_swapped)
    ka_ref[...] = _rope(_head_norm(ka2, gsum[0:LANES, 0:LANES], gka_ref[...]), cos_t, sin_t).astype(BF16)
    va_ref[...] = jnp.where(lo_half, kv_swapped, 1.0).astype(BF16)
    ki_ref[...] = _rope(jnp.where(lo_half, kw, kw_swapped), cos_t, sin_t).astype(BF16)

    qi_ref[...] = _rope(_dot(u, wqi_ref[...]), cos_t, sin_t).astype(BF16)
    wi_t = kw.T
    wi_ref[...] = wi_t[D_I:D_I + SUBLANES, :] * (H_I ** -0.5 * D_I ** -0.5)

    qb = _rope(_head_norm(_dot(u, wqb_ref[...]), gsum, gqb_ref[...]), cos_t, sin_t)
    qb_ref[...] = (qb * (D_B ** -0.5 * LOG2E)).astype(BF16)
    kb = _rope(_head_norm(_dot(u, wkb_ref[...]), gsum, gkb_ref[...]), cos_t, sin_t)
    kb_ref[...] = kb.astype(BF16)
    vb_ref[...] = _dot(u, wvb_ref[...]).astype(BF16)


def _in_proj(h, g, w, tabs, seq_len, tm):
    t, d = h.shape
    nseq = seq_len // tm

    def full(a):
        return pl.BlockSpec(a.shape, lambda i: (0,) * a.ndim)

    def rows(width):
        return pl.BlockSpec((tm, width), lambda i: (i, 0))

    tab = pl.BlockSpec((tm, LANES), lambda i: (i % nseq, 0))
    ins = [h, g, w["qa"], w["sm"], w["qi"], w["qb"], w["kb"], w["vb"],
           tabs["cos"], tabs["sin"], tabs["gsum"], w["gqa"], w["gka"], w["gqb"], w["gkb"]]
    in_specs = [rows(d), full(g)] + [full(a) for a in ins[2:8]] + [tab, tab] + [full(a) for a in ins[10:]]
    widths = [H_A * D_A, LANES, LANES, H_I * D_I, LANES, None, 2 * H_B * D_B, 2 * H_B * D_B, 2 * H_B * D_B]
    out_shape, out_specs = [], []
    for wd in widths:
        if wd is None:
            out_shape.append(jax.ShapeDtypeStruct((SUBLANES, t), F32))
            out_specs.append(pl.BlockSpec((SUBLANES, tm), lambda i: (0, i)))
        else:
            out_shape.append(jax.ShapeDtypeStruct((t, wd), BF16))
            out_specs.append(rows(wd))
    return pl.pallas_call(
        _in_proj_kernel, grid=(t // tm,), in_specs=in_specs, out_specs=out_specs, out_shape=out_shape,
        compiler_params=_cparams(1), name="in_proj")(*ins)


def _key_to_float(k):
    return lax.bitcast_convert_type(k ^ ((k >> 31) & 0x7FFFFFFF), F32)


def _dsa_kernel(qa_ref, qi_ref, wi_ref, ki_ref, ka_ref, va_ref, ltri_ref, o_ref,
                score_ref, coarse_ref, bias_ref, qs_ref, m_ref, acc_ref, perq_ref,
                *, tq, tk, k_top):
    j = pl.program_id(1)
    nkb = (j + 1) * (tq // tk)
    late = tq - tk
    nh = H_A
    groups = tk // SUBLANES
    groups16 = tk // PACKED_ROWS
    no_limit = 2 ** 30
    ways = 4

    lane_q = lax.broadcasted_iota(jnp.int32, (tq, LANES), 1)
    lo_half = lane_q < 64

    def over_blocks(block_fn, init):
        carry = lax.fori_loop(0, nkb - 1, lambda kb, c: block_fn(kb, 0, c), init)
        return block_fn(nkb - 1, late, carry)

    def lane_add(acc, part, q0):
        return acc + part if q0 == 0 else jnp.concatenate([acc[:, :q0], acc[:, q0:] + part], axis=1)

    qi = qi_ref[...]
    qi_heads = []
    for h in range(H_I):
        pair = qi[:, (h // 2) * LANES:(h // 2 + 1) * LANES]
        keep = lo_half if h % 2 == 0 else jnp.logical_not(lo_half)
        qi_heads.append(jnp.where(keep, pair, jnp.zeros_like(pair)))
    wi = wi_ref[...]

    def causal(kb, q0):
        s_idx = kb * tk + lax.broadcasted_iota(jnp.int32, (tk, tq - q0), 0)
        t_idx = j * tq + q0 + lax.broadcasted_iota(jnp.int32, (tk, tq - q0), 1)
        return s_idx <= t_idx

    def score_block(kb, q0, carry):
        kblk = ki_ref[pl.ds(pl.multiple_of(kb * tk, tk), tk), :]
        acc = jnp.zeros((tk, tq - q0), F32)
        for h in range(H_I):
            acc = acc + wi[:, q0:][h:h + 1] * jnp.maximum(_nt_dot(kblk, qi_heads[h][q0:, :]), 0.0)
        sc = jnp.where(causal(kb, q0), acc, -jnp.inf)
        score_ref[kb, :, :, q0:] = sc.reshape(groups, SUBLANES, tq - q0)
        coarse_ref[kb, :, :, q0:] = sc.astype(BF16).reshape(groups16, PACKED_ROWS, tq - q0)
        return carry

    over_blocks(score_block, 0)

    def count(cmp, cand):
        c8 = jnp.broadcast_to(cand, (SUBLANES, tq))

        def block(kb, q0, acc):
            hit = cmp(score_ref[kb, :, :, q0:], c8[:, q0:][None]).astype(jnp.int32)
            part = groups // ways
            sums = [jnp.sum(hit[w * part:(w + 1) * part], axis=0) for w in range(ways)]
            return lane_add(acc, (sums[0] + sums[1]) + (sums[2] + sums[3]), q0)
        acc = over_blocks(block, jnp.zeros((SUBLANES, tq), jnp.int32))
        return jnp.sum(acc, axis=0, keepdims=True)

    def count_coarse(cand16):
        one, zero = jnp.ones((), BF16), jnp.zeros((), BF16)
        c16 = jnp.broadcast_to(cand16, (PACKED_ROWS, tq))

        def block(kb, q0, acc):
            hit = jnp.where(coarse_ref[kb, :, :, q0:] >= c16[:, q0:][None], one, zero)
            part = groups16 // ways
            sums = [functools.reduce(lambda a, b: a + b, [hit[w * part + r] for r in range(part)])
                    for w in range(ways)]
            return lane_add(acc, ((sums[0] + sums[1]) + (sums[2] + sums[3])).astype(F32), q0)
        acc = over_blocks(block, jnp.zeros((PACKED_ROWS, tq), F32))
        return jnp.sum(acc, axis=0, keepdims=True)

    def coarse_body(step, thr16):
        cand = jnp.where(step == 0, 0, thr16 | jnp.left_shift(1, jnp.maximum(15 - step, 0)))
        bits = jnp.left_shift(cand ^ ((cand >> 15) & 0x7FFF), 16)
        n_ge = count_coarse(lax.bitcast_convert_type(bits, F32).astype(BF16))
        return jnp.where(n_ge >= k_top, cand, thr16)

    thr16 = lax.fori_loop(0, 16, coarse_body, jnp.full((1, tq), -(2 ** 15), jnp.int32))
    base = jnp.left_shift(thr16, 16)
    lo_key = jnp.where(base < INT_MIN + 2 ** 16, INT_MIN, base - 2 ** 16)

    def fine_body(step, thr_key):
        cand = thr_key + jnp.left_shift(1, 17 - step)
        cand = jnp.where((thr_key > 0) & (cand < 0), INT_MAX, cand)
        n_ge = count(operator.ge, _key_to_float(cand))
        return jnp.where(n_ge >= k_top, cand, thr_key)

    thr_key = lax.fori_loop(0, 18, fine_body, lo_key)
    n_valid = j * tq + lax.broadcasted_iota(jnp.int32, (1, tq), 1) + 1
    take_all = n_valid <= k_top
    thr = jnp.where(take_all, -jnp.inf, _key_to_float(thr_key))
    n_gt = count(operator.gt, thr)
    need = jnp.where(take_all, no_limit, k_top - n_gt).astype(F32)

    perq_ref[0] = jnp.broadcast_to(thr, (SUBLANES, tq))
    perq_ref[1] = jnp.broadcast_to(need, (SUBLANES, tq))
    perq_ref[2] = jnp.zeros((SUBLANES, tq), F32)

    def bias_block(kb, q0, carry):
        sc = score_ref[kb, :, :, q0:].reshape(tk, tq - q0)
        th, nd, ties_before = perq_ref[0, 0:1, q0:], perq_ref[1, 0:1, q0:], perq_ref[2, 0:1, q0:]
        eq = sc == th
        incl = _dot(ltri_ref[...], jnp.where(eq, 1.0, 0.0).astype(BF16))
        sel = (sc > th) | (eq & (incl + ties_before <= nd))
        sel = sel & causal(kb, q0)
        bias_ref[kb, q0:, :] = jnp.where(sel, 0.0, NEG_MASKED).astype(F32).T
        perq_ref[2, :, q0:] = jnp.broadcast_to(ties_before + incl[tk - 1:tk, :], (SUBLANES, tq - q0))
        return carry

    over_blocks(bias_block, 0)

    qa = qa_ref[...]
    for h in range(nh):
        pair = qa[:, (h // 2) * LANES:(h // 2 + 1) * LANES]
        keep = lo_half if h % 2 == 0 else jnp.logical_not(lo_half)
        qs_ref[h * tq:(h + 1) * tq, :] = jnp.where(keep, pair, jnp.zeros_like(pair))
    m_ref[...] = jnp.full(m_ref.shape, NEG_INIT, F32)
    acc_ref[...] = jnp.zeros(acc_ref.shape, F32)

    def attn_block(kb, q0, carry):
        k0 = pl.multiple_of(kb * tk, tk)
        kblk = ka_ref[pl.ds(k0, tk), :]
        vblk = va_ref[pl.ds(k0, tk), :]
        for h in range(nh):
            rows = slice(h * tq + q0, (h + 1) * tq)
            s = _nt_dot(qs_ref[rows, :], kblk) + bias_ref[kb, q0:, :]
            m_prev = m_ref[rows, :]
            m_new = jnp.maximum(m_prev, jnp.max(s, axis=1, keepdims=True))
            p = jnp.exp2(s - jnp.tile(m_new, (1, tk // LANES)))
            acc_ref[rows, :] = acc_ref[rows, :] * jnp.exp2(m_prev - m_new) + _dot(p.astype(BF16), vblk)
            m_ref[rows, :] = m_new
        return carry

    over_blocks(attn_block, 0)

    for pr in range(nh // 2):
        even = acc_ref[2 * pr * tq:(2 * pr + 1) * tq, :]
        odd = acc_ref[(2 * pr + 1) * tq:(2 * pr + 2) * tq, :]
        num = jnp.where(lo_half, even, pltpu.roll(odd, 64, 1))
        den = jnp.where(lo_half, pltpu.roll(even, 64, 1), odd)
        o_ref[:, pr * LANES:(pr + 1) * LANES] = (num / den).astype(BF16)


def _dsa(qa, qi, wi_t, ki, ka, va, tq, tk):
    b, l, _ = qa.shape
    nq = l // tq
    k_top = min(TOPK_MAX, l // 4)
    kern = functools.partial(_dsa_kernel, tq=tq, tk=tk, k_top=k_top)
    per_q = lambda w: pl.BlockSpec((None, tq, w), lambda bi, j: (bi, j, 0))
    per_b = pl.BlockSpec((None, l, LANES), lambda bi, j: (bi, 0, 0))
    ltri = jnp.tril(jnp.ones((tk, tk), F32)).astype(BF16)
    return pl.pallas_call(
        kern, grid=(b, nq),
        in_specs=[per_q(H_A * D_A), per_q(H_I * D_I),
                  pl.BlockSpec((SUBLANES, tq), lambda bi, j: (0, bi * nq + j)),
                  per_b, per_b, per_b,
                  pl.BlockSpec((tk, tk), lambda bi, j: (0, 0))],
        out_specs=per_q(H_A * D_A),
        out_shape=jax.ShapeDtypeStruct((b, l, H_A * D_A), BF16),
        scratch_shapes=[
            pltpu.VMEM((l // tk, tk // SUBLANES, SUBLANES, tq), F32),
            pltpu.VMEM((l // tk, tk // PACKED_ROWS, PACKED_ROWS, tq), BF16),
            pltpu.VMEM((l // tk, tq, tk), F32),
            pltpu.VMEM((H_A * tq, LANES), BF16),
            pltpu.VMEM((H_A * tq, LANES), F32),
            pltpu.VMEM((H_A * tq, LANES), F32),
            pltpu.VMEM((3, SUBLANES, tq), F32),
        ],
        compiler_params=_cparams(2), name="dsa")(qa, qi, wi_t, ki, ka, va, ltri)


def _diff_kernel(qb_ref, kb_ref, vb_ref, lam_ref, gsub_ref, o_ref, qs_ref, m_ref, l_ref, acc_ref,
                 *, tq, tk, lam_init):
    j = pl.program_id(1)
    diag_blocks = tq // tk
    lamv = lam_ref[...]
    lam = (jnp.exp(jnp.sum(lamv[0:1] * lamv[1:2], axis=1, keepdims=True))
           - jnp.exp(jnp.sum(lamv[2:3] * lamv[3:4], axis=1, keepdims=True)) + lam_init)
    lane_q = lax.broadcasted_iota(jnp.int32, (tq, LANES), 1)
    lo_half = lane_q < 64
    row = lax.broadcasted_iota(jnp.int32, (2 * tq, tk), 0)
    t_loc = jnp.where(row >= tq, row - tq, row)
    s_loc = lax.broadcasted_iota(jnp.int32, (2 * tq, tk), 1)

    for h in range(H_B):
        pair = qb_ref[:, h * LANES:(h + 1) * LANES]
        qs_ref[h, 0:tq, :] = jnp.where(lo_half, pair, jnp.zeros_like(pair))
        qs_ref[h, tq:2 * tq, :] = jnp.where(lo_half, jnp.zeros_like(pair), pair)
    m_ref[...] = jnp.full(m_ref.shape, NEG_INIT, F32)
    l_ref[...] = jnp.zeros(l_ref.shape, F32)
    acc_ref[...] = jnp.zeros(acc_ref.shape, F32)

    def step(kb, diag):
        k0 = pl.multiple_of(kb * tk, tk)
        for h in range(H_B):
            cols = slice(h * LANES, (h + 1) * LANES)
            s = _nt_dot(qs_ref[h], kb_ref[pl.ds(k0, tk), cols])
            if diag is not None:
                s = jnp.where(diag * tk + s_loc <= t_loc, s, NEG_MASKED)
            m_prev = m_ref[h]
            m_new = jnp.maximum(m_prev, jnp.max(s, axis=1, keepdims=True))
            alpha = jnp.exp2(m_prev - m_new)
            p = jnp.exp2(s - jnp.tile(m_new, (1, tk // LANES)))
            l_ref[h] = alpha * l_ref[h] + jnp.sum(p, axis=1, keepdims=True)
            acc_ref[h] = alpha * acc_ref[h] + _dot(p.astype(BF16), vb_ref[pl.ds(k0, tk), cols])
            m_ref[h] = m_new

    def body(kb, carry):
        step(kb, None)
        return carry

    lax.fori_loop(0, j * diag_blocks, body, 0)
    for dblk in range(diag_blocks):
        step(j * diag_blocks + dblk, dblk)

    for h in range(H_B):
        o = acc_ref[h] / l_ref[h]
        o = o[0:tq] - lam * o[tq:2 * tq]
        y = _rms_rows(o, gsub_ref[...]) * (1.0 - lam_init)
        o_ref[:, h * LANES:(h + 1) * LANES] = y.astype(BF16)


def _diffattn(qb, kb, vb, lam_p, gsub, tq, tk, lam_init):
    b, l, w = qb.shape
    nq = l // tq
    kern = functools.partial(_diff_kernel, tq=tq, tk=tk, lam_init=lam_init)
    per_q = pl.BlockSpec((None, tq, w), lambda bi, j: (bi, j, 0))
    per_b = pl.BlockSpec((None, l, w), lambda bi, j: (bi, 0, 0))
    full = lambda a: pl.BlockSpec(a.shape, lambda bi, j: (0,) * a.ndim)
    return pl.pallas_call(
        kern, grid=(b, nq),
        in_specs=[per_q, per_b, per_b, full(lam_p), full(gsub)],
        out_specs=per_q,
        out_shape=jax.ShapeDtypeStruct((b, l, w), BF16),
        scratch_shapes=[pltpu.VMEM((H_B, 2 * tq, LANES), BF16),
                        pltpu.VMEM((H_B, 2 * tq, LANES), F32),
                        pltpu.VMEM((H_B, 2 * tq, LANES), F32),
                        pltpu.VMEM((H_B, 2 * tq, LANES), F32)],
        compiler_params=_cparams(2), name="diffattn")(qb, kb, vb, lam_p, gsub)


def _merge_kernel(h_ref, g_ref, oa_ref, ob_ref, wg_ref, wa_ref, wb_ref, wo_ref, out_ref):
    h = h_ref[...]
    d = h.shape[-1]
    u = _rms_rows(h, g_ref[...]).astype(BF16)
    gates = jax.nn.sigmoid(_dot(u, wg_ref[...]))
    mix = (gates[:, 0:d] * _dot(oa_ref[...], wa_ref[...])
           + gates[:, d:2 * d] * _dot(ob_ref[...], wb_ref[...]))
    out_ref[...] = h + _dot(mix.astype(BF16), wo_ref[...])


def _merge(h, g, oa, ob, wg, wa, wb, wo, tm):
    t, d = h.shape
    rows = lambda w: pl.BlockSpec((tm, w), lambda i: (i, 0))
    full = lambda a: pl.BlockSpec(a.shape, lambda i: (0,) * a.ndim)
    return pl.pallas_call(
        _merge_kernel, grid=(t // tm,),
        in_specs=[rows(d), full(g), rows(oa.shape[1]), rows(ob.shape[1]), full(wg), full(wa), full(wb), full(wo)],
        out_specs=rows(d), out_shape=jax.ShapeDtypeStruct((t, d), F32),
        compiler_params=_cparams(1), name="merge")(h, g, oa, ob, wg, wa, wb, wo)


def _gelu_tanh(x):
    return 0.5 * x * (1.0 + jnp.tanh(math.sqrt(2.0 / math.pi) * (x + 0.044715 * (x * x * x))))


def _ffn_ple_kernel(h_ref, hn_ref, g_ref, wu_ref, cp_ref, wd_ref, p_ref, gp_ref, wpg_ref, wpp_ref, out_ref,
                    tail_ref, xa_ref, xb_ref, xn_ref, acc_ref, *, tiles_per_seq, cw):
    i = pl.program_id(0)
    tm = h_ref.shape[0]
    d_ff = wd_ref.shape[0]
    nc = d_ff // cw
    npan = cw // LANES
    h = h_ref[...]
    acc_ref[...] = jnp.zeros(acc_ref.shape, F32)

    def up_proj(u, c, x_ref, seq_start):
        for half, base in enumerate((0, d_ff)):
            cols = pl.ds(pl.multiple_of(base + c * cw, LANES), cw)
            x = _dot(u, wu_ref[:, cols])
            prev = jnp.where(seq_start, 0.0, tail_ref[:, cols])
            tail_ref[:, cols] = x[tm - SUBLANES:tm, :]
            for p in range(npan):
                x_ref[half * npan + p, 0:SUBLANES, :] = prev[:, p * LANES:(p + 1) * LANES]
                x_ref[half * npan + p, SUBLANES:SUBLANES + tm, :] = x[:, p * LANES:(p + 1) * LANES]

    def gated(c, x_ref):
        outs = []
        for p in range(npan):
            conv = []
            for half, base in enumerate((0, d_ff)):
                taps = cp_ref[:, pl.ds(pl.multiple_of(base + c * cw + p * LANES, LANES), LANES)]
                q = half * npan + p
                conv.append(taps[3:4, :]
                            + x_ref[q, SUBLANES - 2:SUBLANES - 2 + tm, :] * taps[0:1, :]
                            + x_ref[q, SUBLANES - 1:SUBLANES - 1 + tm, :] * taps[1:2, :]
                            + x_ref[q, SUBLANES:SUBLANES + tm, :] * taps[2:3, :])
            outs.append((_gelu_tanh(conv[0]) * conv[1]).astype(BF16))
        return jnp.concatenate(outs, axis=1)

    def down(c, act):
        acc_ref[...] += _dot(act, wd_ref[pl.ds(pl.multiple_of(c * cw, cw), cw), :])

    @pl.when(i == 0)
    def _():
        up_proj(_rms_rows(h, g_ref[...]).astype(BF16), 0, xn_ref, True)

    u = _rms_rows(h, g_ref[...]).astype(BF16)
    seq_start = (i % tiles_per_seq) == 0
    up_proj(u, 1, xa_ref, seq_start)
    down(0, gated(0, xn_ref))
    up_proj(u, 2, xb_ref, seq_start)
    down(1, gated(1, xa_ref))

    def pair(t, carry):
        c = 2 * t + 2
        up_proj(u, c + 1, xa_ref, seq_start)
        down(c, gated(c, xb_ref))
        up_proj(u, c + 2, xb_ref, seq_start)
        down(c + 1, gated(c + 1, xa_ref))
        return carry

    lax.fori_loop(0, (nc - 3) // 2, pair, 0)
    up_proj(_rms_rows(hn_ref[...], g_ref[...]).astype(BF16), 0, xn_ref, ((i + 1) % tiles_per_seq) == 0)
    down(nc - 1, gated(nc - 1, xb_ref))
    h2 = h + acc_ref[...]
    gate = jax.nn.sigmoid(_dot(_rms_rows(h2, gp_ref[...]).astype(BF16), wpg_ref[...]))
    out_ref[...] = h2 + gate * _dot(p_ref[...].astype(BF16), wpp_ref[...])


def _convffn_ple(h, g, wu, cp, wd, p, gp, wpg, wpp, seq_len, tm, cw):
    t, d = h.shape
    n = t // tm
    nc = wd.shape[0] // cw
    assert nc % 2 == 1 and nc >= 3, "the chunk pipeline is written for an odd chunk count"
    rows = lambda w: pl.BlockSpec((tm, w), lambda i: (i, 0))
    next_rows = pl.BlockSpec((tm, d), lambda i: (jnp.minimum(i + 1, n - 1), 0))
    const = lambda a: pl.BlockSpec(a.shape, lambda i: (0,) * a.ndim, pipeline_mode=pl.Buffered(1))
    kern = functools.partial(_ffn_ple_kernel, tiles_per_seq=seq_len // tm, cw=cw)
    panels = pltpu.VMEM((2 * cw // LANES, SUBLANES + tm, LANES), F32)
    return pl.pallas_call(
        kern, grid=(n,),
        in_specs=[rows(d), next_rows, const(g), const(wu), const(cp), const(wd),
                  rows(p.shape[1]), const(gp), const(wpg), const(wpp)],
        out_specs=rows(d), out_shape=jax.ShapeDtypeStruct((t, d), F32),
        scratch_shapes=[pltpu.VMEM((SUBLANES, wu.shape[1]), F32), panels, panels, panels,
                        pltpu.VMEM((tm, d), F32)],
        compiler_params=_cparams(1), name="convffn_ple")(h, h, g, wu, cp, wd, p, gp, wpg, wpp)


def _rope_tables(length):
    inv = 1.0 / (ROPE_THETA ** (jnp.arange(0, D_A, 2, dtype=F32) / D_A))
    ang = jnp.arange(length, dtype=F32)[:, None] * inv[None, :]
    c, s = jnp.cos(ang), jnp.sin(ang)
    return jnp.tile(c, (1, 4)), jnp.tile(jnp.concatenate([-s, s], axis=1), (1, 2))


def _pick_tile(n, pref):
    tile = min(n, pref)
    assert n % tile == 0, (n, tile)
    return tile


def kernel(x, p, g_mix_norm, w_in, g_qa, g_ka, g_qb, g_kb, lam_q1, lam_k1, lam_q2, lam_k2, g_subln,
           w_branch_a, w_branch_b, w_out, g_ffn_norm, w_up, conv_w, conv_b, w_down, g_ple_norm,
           w_ple_gate, w_ple_proj):
    bsz, seq_len, d_model = x.shape
    depth = w_in.shape[0]
    d_ff = w_down.shape[1]
    t = bsz * seq_len
    tq = _pick_tile(seq_len, 512)
    tk = _pick_tile(tq, 256)
    tk_diff = _pick_tile(tq, 512)
    tm = _pick_tile(seq_len, 512)
    cw = 256
    assert d_ff % cw == 0

    cos_t, sin_t = _rope_tables(seq_len)
    gsum = jnp.kron(jnp.eye(4, dtype=F32), jnp.ones((64, 64), F32)).astype(BF16)
    tabs = {"cos": cos_t, "sin": sin_t, "gsum": gsum}

    sizes = [H_A * D_A, D_A, D_A, H_I * D_I, D_I, H_I, 2 * H_B * D_B, 2 * H_B * D_B, 2 * H_B * D_B, d_model, d_model]
    offs = [0]
    for sz in sizes:
        offs.append(offs[-1] + sz)
    assert offs[-1] == w_in.shape[2]

    h = x.reshape(t, d_model)
    for i in range(depth):
        lam_init = 0.8 - 0.6 * math.exp(-0.3 * i)
        segs = lax.optimization_barrier([w_in[i, :, offs[k]:offs[k + 1]] for k in range(len(sizes))])
        seg = lambda k: segs[k]
        wts = {
            "qa": seg(0).astype(BF16),
            "sm": jnp.concatenate([seg(1), seg(2), seg(4),
                                   jnp.pad(seg(5), ((0, 0), (0, D_I - H_I)))], axis=1).astype(BF16),
            "qi": seg(3).astype(BF16),
            "qb": seg(6).astype(BF16), "kb": seg(7).astype(BF16), "vb": seg(8).astype(BF16),
            "gqa": jnp.tile(g_qa[i], H_A)[None, :], "gka": jnp.tile(g_ka[i], 2)[None, :],
            "gqb": jnp.tile(g_qb[i], 2 * H_B)[None, :], "gkb": jnp.tile(g_kb[i], 2 * H_B)[None, :],
        }
        qa, ka, va, qi, ki, wi_t, qb, kb, vb = _in_proj(h, g_mix_norm[i][None, :], wts, tabs, seq_len, tm)
        r3 = lambda a: a.reshape(bsz, seq_len, a.shape[-1])
        o_a = _dsa(r3(qa), r3(qi), wi_t, r3(ki), r3(ka), r3(va), tq, tk)
        lam_p = jnp.stack([lam_q1[i], lam_k1[i], lam_q2[i], lam_k2[i]])
        o_b = _diffattn(r3(qb), r3(kb), r3(vb), lam_p, g_subln[i][None, :], tq, tk_diff, lam_init)

        w_gate = jnp.concatenate([seg(9), seg(10)], axis=1).astype(BF16)
        h = _merge(h, g_mix_norm[i][None, :], o_a.reshape(t, -1), o_b.reshape(t, -1), w_gate,
                   w_branch_a[i].astype(BF16), w_branch_b[i].astype(BF16), w_out[i].astype(BF16), tm)

        conv = jnp.concatenate([conv_w[i], conv_b[i][None, :]], axis=0)
        conv = jnp.pad(conv, ((0, SUBLANES - CONV_W - 1), (0, 0)))
        h = _convffn_ple(h, g_ffn_norm[i][None, :], w_up[i].astype(BF16), conv, w_down[i].astype(BF16),
                         p[i].reshape(t, -1), g_ple_norm[i][None, :], w_ple_gate[i].astype(BF16),
                         w_ple_proj[i].astype(BF16), seq_len, tm, cw)
    return h.reshape(bsz, seq_len, d_model)
```

```python
import functools
import math

import jax
import jax.numpy as jnp
from jax import lax
from jax.experimental import pallas as pl
from jax.experimental.pallas import tpu as pltpu

H_A, D_A = 8, 64
H_I, D_I = 4, 64
H_B, D_B = 4, 64
TOPK_MAX = 256
CONV_W = 3
ROPE_THETA = 10000.0
EPS = 1e-6

LANES = 128
SUBLANES = 8
PACKED_ROWS = 16
VMEM_LIMIT = 56 * 1024 * 1024

F32 = jnp.float32
BF16 = jnp.bfloat16
NEG_MASKED = -2e30
NEG_INIT = -1e30
INT_MIN = -(2 ** 31)
INT_MAX = 2 ** 31 - 1
LOG2E = math.log2(math.e)


def _nt_dot(a, b):
    return lax.dot_general(a, b, (((1,), (1,)), ((), ())), preferred_element_type=F32)


def _dot(a, b):
    return jnp.dot(a, b, preferred_element_type=F32)


def _cparams(n_axes):
    return pltpu.CompilerParams(dimension_semantics=("arbitrary",) * n_axes,
                                vmem_limit_bytes=VMEM_LIMIT)


def _rms_rows(x, g):
    ms = jnp.mean(x * x, axis=-1, keepdims=True)
    return x * lax.rsqrt(ms + EPS) * g


def _rope(x, cos_t, sin_t):
    w = x.shape[-1]
    reps = w // LANES
    c = jnp.tile(cos_t, (1, reps)) if reps > 1 else cos_t
    s = jnp.tile(sin_t, (1, reps)) if reps > 1 else sin_t
    up = pltpu.roll(x, w - 32, 1)
    dn = pltpu.roll(x, 32, 1)
    lane = lax.broadcasted_iota(jnp.int32, x.shape, 1)
    partner = jnp.where((lane & 63) < 32, up, dn)
    return x * c + partner * s


def _head_norm(x, gsum, g):
    slab = gsum.shape[0]
    sq = (x * x).astype(BF16)
    parts = [_dot(sq[:, c:c + slab], gsum) for c in range(0, x.shape[-1], slab)]
    ss = (parts[0] if len(parts) == 1 else jnp.concatenate(parts, axis=1)) * (1.0 / 64.0)
    return x * lax.rsqrt(ss + EPS) * g


def _in_proj_kernel(h_ref, g_ref, wqa_ref, wsm_ref, wqi_ref, wqb_ref, wkb_ref, wvb_ref,
                    cos_ref, sin_ref, gsum_ref, gqa_ref, gka_ref, gqb_ref, gkb_ref,
                    qa_ref, ka_ref, va_ref, qi_ref, ki_ref, wi_ref, qb_ref, kb_ref, vb_ref):
    u = _rms_rows(h_ref[...], g_ref[...]).astype(BF16)
    cos_t, sin_t = cos_ref[...], sin_ref[...]
    gsum = gsum_ref[...]

    qa = _rope(_head_norm(_dot(u, wqa_ref[...]), gsum, gqa_ref[...]), cos_t, sin_t)
    qa_ref[...] = (qa * (D_A ** -0.5 * LOG2E)).astype(BF16)

    sm = _dot(u, wsm_ref[...])
    kv, kw = sm[:, 0:LANES], sm[:, LANES:2 * LANES]
    kv_swapped, kw_swapped = pltpu.roll(kv, D_A, 1), pltpu.roll(kw, D_I, 1)
    lo_half = lax.broadcasted_iota(jnp.int32, (sm.shape[0], LANES), 1) < D_A
    ka2 = jnp.where(lo_half, kv, kv_swapped)
    ka_ref[...] = _rope(_head_norm(ka2, gsum[0:LANES, 0:LANES], gka_ref[...]), cos_t, sin_t).astype(BF16)
    va_ref[...] = jnp.where(lo_half, kv_swapped, 1.0).astype(BF16)
    ki_ref[...] = _rope(jnp.where(lo_half, kw, kw_swapped), cos_t, sin_t).astype(BF16)

    qi_ref[...] = _rope(_dot(u, wqi_ref[...]), cos_t, sin_t).astype(BF16)
    wi_t = kw.T
    wi_ref[...] = wi_t[D_I:D_I + SUBLANES, :] * (H_I ** -0.5 * D_I ** -0.5)

    qb = _rope(_head_norm(_dot(u, wqb_ref[...]), gsum, gqb_ref[...]), cos_t, sin_t)
    qb_ref[...] = (qb * (D_B ** -0.5 * LOG2E)).astype(BF16)
    kb = _rope(_head_norm(_dot(u, wkb_ref[...]), gsum, gkb_ref[...]), cos_t, sin_t)
    kb_ref[...] = kb.astype(BF16)
    vb_ref[...] = _dot(u, wvb_ref[...]).astype(BF16)


def _in_proj(h, g, w, tabs, seq_len, tm):
    t, d = h.shape
    nseq = seq_len // tm

    def full(a):
        return pl.BlockSpec(a.shape, lambda i: (0,) * a.ndim)

    def rows(width):
        return pl.BlockSpec((tm, width), lambda i: (i, 0))

    tab = pl.BlockSpec((tm, LANES), lambda i: (i % nseq, 0))
    ins = [h, g, w["qa"], w["sm"], w["qi"], w["qb"], w["kb"], w["vb"],
           tabs["cos"], tabs["sin"], tabs["gsum"], w["gqa"], w["gka"], w["gqb"], w["gkb"]]
    in_specs = [rows(d), full(g)] + [full(a) for a in ins[2:8]] + [tab, tab] + [full(a) for a in ins[10:]]
    widths = [H_A * D_A, LANES, LANES, H_I * D_I, LANES, None, 2 * H_B * D_B, 2 * H_B * D_B, 2 * H_B * D_B]
    out_shape, out_specs = [], []
    for wd in widths:
        if wd is None:
            out_shape.append(jax.ShapeDtypeStruct((SUBLANES, t), F32))
            out_specs.append(pl.BlockSpec((SUBLANES, tm), lambda i: (0, i)))
        else:
            out_shape.append(jax.ShapeDtypeStruct((t, wd), BF16))
            out_specs.append(rows(wd))
    return pl.pallas_call(
        _in_proj_kernel, grid=(t // tm,), in_specs=in_specs, out_specs=out_specs, out_shape=out_shape,
        compiler_params=_cparams(1), name="in_proj")(*ins)


def _key_to_float(k):
    return lax.bitcast_convert_type(k ^ ((k >> 31) & 0x7FFFFFFF), F32)


def _dsa_kernel(qa_ref, qi_ref, wi_ref, ki_ref, ka_ref, va_ref, ltri_ref, o_ref,
                score_ref, coarse_ref, bias_ref, qs_ref, m_ref, acc_ref,
                *, tq, tk, k_top):
    j = pl.program_id(1)
    nkb = (j + 1) * (tq // tk)
    nh = H_A
    groups = tk // SUBLANES
    groups16 = tk // PACKED_ROWS
    no_limit = 2 ** 30
    per_trip = 2 if (tq // tk) % 2 == 0 else 1

    def for_blocks(block_fn, init):
        def trip(t, carry):
            for r in range(per_trip):
                carry = block_fn(t * per_trip + r, carry)
            return carry
        return lax.fori_loop(0, nkb // per_trip, trip, init)

    lane_q = lax.broadcasted_iota(jnp.int32, (tq, LANES), 1)
    lo_half = lane_q < 64

    qi = qi_ref[...]
    qi_heads = []
    for h in range(H_I):
        pair = qi[:, (h // 2) * LANES:(h // 2 + 1) * LANES]
        keep = lo_half if h % 2 == 0 else jnp.logical_not(lo_half)
        qi_heads.append(jnp.where(keep, pair, jnp.zeros_like(pair)))
    wi = wi_ref[...]
    t_idx = j * tq + lax.broadcasted_iota(jnp.int32, (tk, tq), 1)
    s_loc = lax.broadcasted_iota(jnp.int32, (tk, tq), 0)

    def score_body(kb, carry):
        kblk = ki_ref[pl.ds(pl.multiple_of(kb * tk, tk), tk), :]
        acc = jnp.zeros((tk, tq), F32)
        for h in range(H_I):
            acc = acc + wi[h:h + 1, :] * jnp.maximum(_nt_dot(kblk, qi_heads[h]), 0.0)
        sc = jnp.where(kb * tk + s_loc <= t_idx, acc, -jnp.inf)
        score_ref[kb] = sc.reshape(groups, SUBLANES, tq)
        coarse_ref[kb] = sc.astype(BF16).reshape(groups16, PACKED_ROWS, tq)
        return carry

    for_blocks(score_body, 0)

    ways = 4

    def count(pred):
        def body(kb, acc):
            hit = pred(score_ref[kb]).astype(jnp.int32)
            part = groups // ways
            sums = [jnp.sum(hit[w * part:(w + 1) * part], axis=0) for w in range(ways)]
            return acc + ((sums[0] + sums[1]) + (sums[2] + sums[3]))
        acc = for_blocks(body, jnp.zeros((SUBLANES, tq), jnp.int32))
        return jnp.sum(acc, axis=0, keepdims=True)

    def count_coarse(c16):
        one, zero = jnp.ones((), BF16), jnp.zeros((), BF16)

        def body(kb, acc):
            hit = jnp.where(coarse_ref[kb] >= c16, one, zero)
            part = groups16 // ways
            sums = [functools.reduce(lambda a, b: a + b, [hit[w * part + r] for r in range(part)])
                    for w in range(ways)]
            return acc + ((sums[0] + sums[1]) + (sums[2] + sums[3])).astype(F32)
        acc = for_blocks(body, jnp.zeros((PACKED_ROWS, tq), F32))
        return jnp.sum(acc, axis=0, keepdims=True)

    def bcast8(v):
        return jnp.broadcast_to(v, (SUBLANES, tq))[None]

    def coarse_body(step, thr16):
        cand = jnp.where(step == 0, 0, thr16 | jnp.left_shift(1, jnp.maximum(15 - step, 0)))
        bits = jnp.left_shift(cand ^ ((cand >> 15) & 0x7FFF), 16)
        c16 = lax.bitcast_convert_type(bits, F32).astype(BF16)
        n_ge = count_coarse(jnp.broadcast_to(c16, (PACKED_ROWS, tq))[None])
        return jnp.where(n_ge >= k_top, cand, thr16)

    thr16 = lax.fori_loop(0, 16, coarse_body, jnp.full((1, tq), -(2 ** 15), jnp.int32))
    base = jnp.left_shift(thr16, 16)
    lo_key = jnp.where(base < INT_MIN + 2 ** 16, INT_MIN, base - 2 ** 16)

    def fine_body(step, thr_key):
        cand = thr_key + jnp.left_shift(1, 17 - step)
        cand = jnp.where((thr_key > 0) & (cand < 0), INT_MAX, cand)
        c8 = bcast8(_key_to_float(cand))
        n_ge = count(lambda blk: blk >= c8)
        return jnp.where(n_ge >= k_top, cand, thr_key)

    thr_key = lax.fori_loop(0, 18, fine_body, lo_key)
    n_valid = j * tq + lax.broadcasted_iota(jnp.int32, (1, tq), 1) + 1
    take_all = n_valid <= k_top
    thr = jnp.where(take_all, -jnp.inf, _key_to_float(thr_key))
    t8 = bcast8(thr)
    n_gt = count(lambda blk: blk > t8)
    need = jnp.where(take_all, no_limit, k_top - n_gt).astype(F32)

    def bias_body(kb, ties_before):
        sc = score_ref[kb].reshape(tk, tq)
        eq = sc == thr
        incl = _dot(ltri_ref[...], jnp.where(eq, 1.0, 0.0).astype(BF16))
        sel = (sc > thr) | (eq & (incl + ties_before <= need))
        sel = sel & (kb * tk + s_loc <= t_idx)
        bias_ref[kb] = jnp.where(sel, 0.0, NEG_MASKED).astype(F32).T
        return ties_before + incl[tk - 1:tk, :]

    for_blocks(bias_body, jnp.zeros((1, tq), F32))

    qa = qa_ref[...]
    for h in range(nh):
        pair = qa[:, (h // 2) * LANES:(h // 2 + 1) * LANES]
        keep = lo_half if h % 2 == 0 else jnp.logical_not(lo_half)
        qs_ref[h * tq:(h + 1) * tq, :] = jnp.where(keep, pair, jnp.zeros_like(pair))
    m_ref[...] = jnp.full(m_ref.shape, NEG_INIT, F32)
    acc_ref[...] = jnp.zeros(acc_ref.shape, F32)

    def attn_body(kb, carry):
        k0 = pl.multiple_of(kb * tk, tk)
        kblk = ka_ref[pl.ds(k0, tk), :]
        vblk = va_ref[pl.ds(k0, tk), :]
        for h in range(nh):
            rows = slice(h * tq, (h + 1) * tq)
            s = _nt_dot(qs_ref[rows, :], kblk) + bias_ref[kb]
            m_prev = m_ref[rows, :]
            m_new = jnp.maximum(m_prev, jnp.max(s, axis=1, keepdims=True))
            p = jnp.exp2(s - jnp.tile(m_new, (1, tk // LANES)))
            acc_ref[rows, :] = acc_ref[rows, :] * jnp.exp2(m_prev - m_new) + _dot(p.astype(BF16), vblk)
            m_ref[rows, :] = m_new
        return carry

    for_blocks(attn_body, 0)

    for pr in range(nh // 2):
        even = acc_ref[2 * pr * tq:(2 * pr + 1) * tq, :]
        odd = acc_ref[(2 * pr + 1) * tq:(2 * pr + 2) * tq, :]
        num = jnp.where(lo_half, even, pltpu.roll(odd, 64, 1))
        den = jnp.where(lo_half, pltpu.roll(even, 64, 1), odd)
        o_ref[:, pr * LANES:(pr + 1) * LANES] = (num / den).astype(BF16)


def _dsa(qa, qi, wi_t, ki, ka, va, tq, tk):
    b, l, _ = qa.shape
    nq = l // tq
    k_top = min(TOPK_MAX, l // 4)
    kern = functools.partial(_dsa_kernel, tq=tq, tk=tk, k_top=k_top)
    per_q = lambda w: pl.BlockSpec((None, tq, w), lambda bi, j: (bi, j, 0))
    per_b = pl.BlockSpec((None, l, LANES), lambda bi, j: (bi, 0, 0))
    ltri = jnp.tril(jnp.ones((tk, tk), F32)).astype(BF16)
    return pl.pallas_call(
        kern, grid=(b, nq),
        in_specs=[per_q(H_A * D_A), per_q(H_I * D_I),
                  pl.BlockSpec((SUBLANES, tq), lambda bi, j: (0, bi * nq + j)),
                  per_b, per_b, per_b,
                  pl.BlockSpec((tk, tk), lambda bi, j: (0, 0))],
        out_specs=per_q(H_A * D_A),
        out_shape=jax.ShapeDtypeStruct((b, l, H_A * D_A), BF16),
        scratch_shapes=[
            pltpu.VMEM((l // tk, tk // SUBLANES, SUBLANES, tq), F32),
            pltpu.VMEM((l // tk, tk // PACKED_ROWS, PACKED_ROWS, tq), BF16),
            pltpu.VMEM((l // tk, tq, tk), F32),
            pltpu.VMEM((H_A * tq, LANES), BF16),
            pltpu.VMEM((H_A * tq, LANES), F32),
            pltpu.VMEM((H_A * tq, LANES), F32),
        ],
        compiler_params=_cparams(2), name="dsa")(qa, qi, wi_t, ki, ka, va, ltri)


def _diff_kernel(qb_ref, kb_ref, vb_ref, lam_ref, gsub_ref, o_ref, qs_ref, m_ref, l_ref, acc_ref,
                 *, tq, tk, lam_init):
    j = pl.program_id(1)
    diag_blocks = tq // tk
    lamv = lam_ref[...]
    lam = (jnp.exp(jnp.sum(lamv[0:1] * lamv[1:2], axis=1, keepdims=True))
           - jnp.exp(jnp.sum(lamv[2:3] * lamv[3:4], axis=1, keepdims=True)) + lam_init)
    lane_q = lax.broadcasted_iota(jnp.int32, (tq, LANES), 1)
    lo_half = lane_q < 64
    row = lax.broadcasted_iota(jnp.int32, (2 * tq, tk), 0)
    t_loc = jnp.where(row >= tq, row - tq, row)
    s_loc = lax.broadcasted_iota(jnp.int32, (2 * tq, tk), 1)

    for h in range(H_B):
        pair = qb_ref[:, h * LANES:(h + 1) * LANES]
        qs_ref[h, 0:tq, :] = jnp.where(lo_half, pair, jnp.zeros_like(pair))
        qs_ref[h, tq:2 * tq, :] = jnp.where(lo_half, jnp.zeros_like(pair), pair)
    m_ref[...] = jnp.full(m_ref.shape, NEG_INIT, F32)
    l_ref[...] = jnp.zeros(l_ref.shape, F32)
    acc_ref[...] = jnp.zeros(acc_ref.shape, F32)

    def step(kb, diag):
        k0 = pl.multiple_of(kb * tk, tk)
        for h in range(H_B):
            cols = slice(h * LANES, (h + 1) * LANES)
            s = _nt_dot(qs_ref[h], kb_ref[pl.ds(k0, tk), cols])
            if diag is not None:
                s = jnp.where(diag * tk + s_loc <= t_loc, s, NEG_MASKED)
            m_prev = m_ref[h]
            m_new = jnp.maximum(m_prev, jnp.max(s, axis=1, keepdims=True))
            alpha = jnp.exp2(m_prev - m_new)
            p = jnp.exp2(s - jnp.tile(m_new, (1, tk // LANES)))
            l_ref[h] = alpha * l_ref[h] + jnp.sum(p, axis=1, keepdims=True)
            acc_ref[h] = alpha * acc_ref[h] + _dot(p.astype(BF16), vb_ref[pl.ds(k0, tk), cols])
            m_ref[h] = m_new

    def body(kb, carry):
        step(kb, None)
        return carry

    lax.fori_loop(0, j * diag_blocks, body, 0)
    for dblk in range(diag_blocks):
        step(j * diag_blocks + dblk, dblk)

    for h in range(H_B):
        o = acc_ref[h] / l_ref[h]
        o = o[0:tq] - lam * o[tq:2 * tq]
        y = _rms_rows(o, gsub_ref[...]) * (1.0 - lam_init)
        o_ref[:, h * LANES:(h + 1) * LANES] = y.astype(BF16)


def _diffattn(qb, kb, vb, lam_p, gsub, tq, tk, lam_init):
    b, l, w = qb.shape
    nq = l // tq
    kern = functools.partial(_diff_kernel, tq=tq, tk=tk, lam_init=lam_init)
    per_q = pl.BlockSpec((None, tq, w), lambda bi, j: (bi, j, 0))
    per_b = pl.BlockSpec((None, l, w), lambda bi, j: (bi, 0, 0))
    full = lambda a: pl.BlockSpec(a.shape, lambda bi, j: (0,) * a.ndim)
    return pl.pallas_call(
        kern, grid=(b, nq),
        in_specs=[per_q, per_b, per_b, full(lam_p), full(gsub)],
        out_specs=per_q,
        out_shape=jax.ShapeDtypeStruct((b, l, w), BF16),
        scratch_shapes=[pltpu.VMEM((H_B, 2 * tq, LANES), BF16),
                        pltpu.VMEM((H_B, 2 * tq, LANES), F32),
                        pltpu.VMEM((H_B, 2 * tq, LANES), F32),
                        pltpu.VMEM((H_B, 2 * tq, LANES), F32)],
        compiler_params=_cparams(2), name="diffattn")(qb, kb, vb, lam_p, gsub)


def _merge_kernel(h_ref, g_ref, oa_ref, ob_ref, wg_ref, wa_ref, wb_ref, wo_ref, out_ref):
    h = h_ref[...]
    d = h.shape[-1]
    u = _rms_rows(h, g_ref[...]).astype(BF16)
    gates = jax.nn.sigmoid(_dot(u, wg_ref[...]))
    mix = (gates[:, 0:d] * _dot(oa_ref[...], wa_ref[...])
           + gates[:, d:2 * d] * _dot(ob_ref[...], wb_ref[...]))
    out_ref[...] = h + _dot(mix.astype(BF16), wo_ref[...])


def _merge(h, g, oa, ob, wg, wa, wb, wo, tm):
    t, d = h.shape
    rows = lambda w: pl.BlockSpec((tm, w), lambda i: (i, 0))
    full = lambda a: pl.BlockSpec(a.shape, lambda i: (0,) * a.ndim)
    return pl.pallas_call(
        _merge_kernel, grid=(t // tm,),
        in_specs=[rows(d), full(g), rows(oa.shape[1]), rows(ob.shape[1]), full(wg), full(wa), full(wb), full(wo)],
        out_specs=rows(d), out_shape=jax.ShapeDtypeStruct((t, d), F32),
        compiler_params=_cparams(1), name="merge")(h, g, oa, ob, wg, wa, wb, wo)


def _gelu_tanh(x):
    return 0.5 * x * (1.0 + jnp.tanh(math.sqrt(2.0 / math.pi) * (x + 0.044715 * (x * x * x))))


def _ffn_ple_kernel(h_ref, hn_ref, g_ref, wu_ref, cp_ref, wd_ref, p_ref, gp_ref, wpg_ref, wpp_ref, out_ref,
                    tail_ref, xa_ref, xb_ref, xn_ref, acc_ref, *, tiles_per_seq, cw):
    i = pl.program_id(0)
    tm = h_ref.shape[0]
    d_ff = wd_ref.shape[0]
    nc = d_ff // cw
    npan = cw // LANES
    h = h_ref[...]
    acc_ref[...] = jnp.zeros(acc_ref.shape, F32)

    def up_proj(u, c, x_ref, seq_start):
        for half, base in enumerate((0, d_ff)):
            cols = pl.ds(pl.multiple_of(base + c * cw, LANES), cw)
            x = _dot(u, wu_ref[:, cols])
            prev = jnp.where(seq_start, 0.0, tail_ref[:, cols])
            tail_ref[:, cols] = x[tm - SUBLANES:tm, :]
            for p in range(npan):
                x_ref[half * npan + p, 0:SUBLANES, :] = prev[:, p * LANES:(p + 1) * LANES]
                x_ref[half * npan + p, SUBLANES:SUBLANES + tm, :] = x[:, p * LANES:(p + 1) * LANES]

    def gated(c, x_ref):
        outs = []
        for p in range(npan):
            conv = []
            for half, base in enumerate((0, d_ff)):
                taps = cp_ref[:, pl.ds(pl.multiple_of(base + c * cw + p * LANES, LANES), LANES)]
                q = half * npan + p
                conv.append(taps[3:4, :]
                            + x_ref[q, SUBLANES - 2:SUBLANES - 2 + tm, :] * taps[0:1, :]
                            + x_ref[q, SUBLANES - 1:SUBLANES - 1 + tm, :] * taps[1:2, :]
                            + x_ref[q, SUBLANES:SUBLANES + tm, :] * taps[2:3, :])
            outs.append((_gelu_tanh(conv[0]) * conv[1]).astype(BF16))
        return jnp.concatenate(outs, axis=1)

    def down(c, act):
        acc_ref[...] += _dot(act, wd_ref[pl.ds(pl.multiple_of(c * cw, cw), cw), :])

    @pl.when(i == 0)
    def _():
        up_proj(_rms_rows(h, g_ref[...]).astype(BF16), 0, xn_ref, True)

    u = _rms_rows(h, g_ref[...]).astype(BF16)
    seq_start = (i % tiles_per_seq) == 0
    up_proj(u, 1, xa_ref, seq_start)
    down(0, gated(0, xn_ref))
    up_proj(u, 2, xb_ref, seq_start)
    down(1, gated(1, xa_ref))

    pairs = (nc - 3) // 2
    pairs_per_trip = 2 if pairs % 2 == 0 else 1

    def trip(t, carry):
        for r in range(pairs_per_trip):
            c = 2 * (t * pairs_per_trip + r) + 2
            up_proj(u, c + 1, xa_ref, seq_start)
            down(c, gated(c, xb_ref))
            up_proj(u, c + 2, xb_ref, seq_start)
            down(c + 1, gated(c + 1, xa_ref))
        return carry

    lax.fori_loop(0, pairs // pairs_per_trip, trip, 0)
    up_proj(_rms_rows(hn_ref[...], g_ref[...]).astype(BF16), 0, xn_ref, ((i + 1) % tiles_per_seq) == 0)
    down(nc - 1, gated(nc - 1, xb_ref))
    h2 = h + acc_ref[...]
    gate = jax.nn.sigmoid(_dot(_rms_rows(h2, gp_ref[...]).astype(BF16), wpg_ref[...]))
    out_ref[...] = h2 + gate * _dot(p_ref[...].astype(BF16), wpp_ref[...])


def _convffn_ple(h, g, wu, cp, wd, p, gp, wpg, wpp, seq_len, tm, cw):
    t, d = h.shape
    n = t // tm
    nc = wd.shape[0] // cw
    assert nc % 2 == 1 and nc >= 3, "the chunk pipeline is written for an odd chunk count"
    rows = lambda w: pl.BlockSpec((tm, w), lambda i: (i, 0))
    next_rows = pl.BlockSpec((tm, d), lambda i: (jnp.minimum(i + 1, n - 1), 0))
    const = lambda a: pl.BlockSpec(a.shape, lambda i: (0,) * a.ndim, pipeline_mode=pl.Buffered(1))
    kern = functools.partial(_ffn_ple_kernel, tiles_per_seq=seq_len // tm, cw=cw)
    panels = pltpu.VMEM((2 * cw // LANES, SUBLANES + tm, LANES), F32)
    return pl.pallas_call(
        kern, grid=(n,),
        in_specs=[rows(d), next_rows, const(g), const(wu), const(cp), const(wd),
                  rows(p.shape[1]), const(gp), const(wpg), const(wpp)],
        out_specs=rows(d), out_shape=jax.ShapeDtypeStruct((t, d), F32),
        scratch_shapes=[pltpu.VMEM((SUBLANES, wu.shape[1]), F32), panels, panels, panels,
                        pltpu.VMEM((tm, d), F32)],
        compiler_params=_cparams(1), name="convffn_ple")(h, h, g, wu, cp, wd, p, gp, wpg, wpp)


def _rope_tables(length):
    inv = 1.0 / (ROPE_THETA ** (jnp.arange(0, D_A, 2, dtype=F32) / D_A))
    ang = jnp.arange(length, dtype=F32)[:, None] * inv[None, :]
    c, s = jnp.cos(ang), jnp.sin(ang)
    return jnp.tile(c, (1, 4)), jnp.tile(jnp.concatenate([-s, s], axis=1), (1, 2))


def _pick_tile(n, pref):
    tile = min(n, pref)
    assert n % tile == 0, (n, tile)
    return tile


def kernel(x, p, g_mix_norm, w_in, g_qa, g_ka, g_qb, g_kb, lam_q1, lam_k1, lam_q2, lam_k2, g_subln,
           w_branch_a, w_branch_b, w_out, g_ffn_norm, w_up, conv_w, conv_b, w_down, g_ple_norm,
           w_ple_gate, w_ple_proj):
    bsz, seq_len, d_model = x.shape
    depth = w_in.shape[0]
    d_ff = w_down.shape[1]
    t = bsz * seq_len
    tq = _pick_tile(seq_len, 512)
    tk = _pick_tile(tq, 256)
    tk_diff = _pick_tile(tq, 512)
    tm = _pick_tile(seq_len, 512)
    cw = 256
    assert d_ff % cw == 0

    cos_t, sin_t = _rope_tables(seq_len)
    gsum = jnp.kron(jnp.eye(4, dtype=F32), jnp.ones((64, 64), F32)).astype(BF16)
    tabs = {"cos": cos_t, "sin": sin_t, "gsum": gsum}

    sizes = [H_A * D_A, D_A, D_A, H_I * D_I, D_I, H_I, 2 * H_B * D_B, 2 * H_B * D_B, 2 * H_B * D_B, d_model, d_model]
    offs = [0]
    for sz in sizes:
        offs.append(offs[-1] + sz)
    assert offs[-1] == w_in.shape[2]

    h = x.reshape(t, d_model)
    for i in range(depth):
        lam_init = 0.8 - 0.6 * math.exp(-0.3 * i)
        segs = lax.optimization_barrier([w_in[i, :, offs[k]:offs[k + 1]] for k in range(len(sizes))])
        seg = lambda k: segs[k]
        wts = {
            "qa": seg(0).astype(BF16),
            "sm": jnp.concatenate([seg(1), seg(2), seg(4),
                                   jnp.pad(seg(5), ((0, 0), (0, D_I - H_I)))], axis=1).astype(BF16),
            "qi": seg(3).astype(BF16),
            "qb": seg(6).astype(BF16), "kb": seg(7).astype(BF16), "vb": seg(8).astype(BF16),
            "gqa": jnp.tile(g_qa[i], H_A)[None, :], "gka": jnp.tile(g_ka[i], 2)[None, :],
            "gqb": jnp.tile(g_qb[i], 2 * H_B)[None, :], "gkb": jnp.tile(g_kb[i], 2 * H_B)[None, :],
        }
        qa, ka, va, qi, ki, wi_t, qb, kb, vb = _in_proj(h, g_mix_norm[i][None, :], wts, tabs, seq_len, tm)
        r3 = lambda a: a.reshape(bsz, seq_len, a.shape[-1])
        o_a = _dsa(r3(qa), r3(qi), wi_t, r3(ki), r3(ka), r3(va), tq, tk)
        lam_p = jnp.stack([lam_q1[i], lam_k1[i], lam_q2[i], lam_k2[i]])
        o_b = _diffattn(r3(qb), r3(kb), r3(vb), lam_p, g_subln[i][None, :], tq, tk_diff, lam_init)

        w_gate = jnp.concatenate([seg(9), seg(10)], axis=1).astype(BF16)
        h = _merge(h, g_mix_norm[i][None, :], o_a.reshape(t, -1), o_b.reshape(t, -1), w_gate,
                   w_branch_a[i].astype(BF16), w_branch_b[i].astype(BF16), w_out[i].astype(BF16), tm)

        conv = jnp.concatenate([conv_w[i], conv_b[i][None, :]], axis=0)
        conv = jnp.pad(conv, ((0, SUBLANES - CONV_W - 1), (0, 0)))
        h = _convffn_ple(h, g_ffn_norm[i][None, :], w_up[i].astype(BF16), conv, w_down[i].astype(BF16),
                         p[i].reshape(t, -1), g_ple_norm[i][None, :], w_ple_gate[i].astype(BF16),
                         w_ple_proj[i].astype(BF16), seq_len, tm, cw)
    return h.reshape(bsz, seq_len, d_model)
```

```python
import functools
import math

import jax
import jax.numpy as jnp
from jax import lax
from jax.experimental import pallas as pl
from jax.experimental.pallas import tpu as pltpu

H_A, D_A = 8, 64
H_I, D_I = 4, 64
H_B, D_B = 4, 64
TOPK_MAX = 256
CONV_W = 3
ROPE_THETA = 10000.0
EPS = 1e-6

LANES = 128
SUBLANES = 8
PACKED_ROWS = 16
VMEM_LIMIT = 56 * 1024 * 1024

F32 = jnp.float32
BF16 = jnp.bfloat16
NEG_MASKED = -2e30
NEG_INIT = -1e30
INT_MIN = -(2 ** 31)
INT_MAX = 2 ** 31 - 1
LOG2E = math.log2(math.e)


def _nt_dot(a, b):
    return lax.dot_general(a, b, (((1,), (1,)), ((), ())), preferred_element_type=F32)


def _dot(a, b):
    return jnp.dot(a, b, preferred_element_type=F32)


def _cparams(n_axes):
    return pltpu.CompilerParams(dimension_semantics=("arbitrary",) * n_axes,
                                vmem_limit_bytes=VMEM_LIMIT)


def _rms_rows(x, g):
    ms = jnp.mean(x * x, axis=-1, keepdims=True)
    return x * lax.rsqrt(ms + EPS) * g


def _rope(x, cos_t, sin_t):
    w = x.shape[-1]
    reps = w // LANES
    c = jnp.tile(cos_t, (1, reps)) if reps > 1 else cos_t
    s = jnp.tile(sin_t, (1, reps)) if reps > 1 else sin_t
    up = pltpu.roll(x, w - 32, 1)
    dn = pltpu.roll(x, 32, 1)
    lane = lax.broadcasted_iota(jnp.int32, x.shape, 1)
    partner = jnp.where((lane & 63) < 32, up, dn)
    return x * c + partner * s


def _head_norm(x, gsum, g):
    slab = gsum.shape[0]
    sq = (x * x).astype(BF16)
    parts = [_dot(sq[:, c:c + slab], gsum) for c in range(0, x.shape[-1], slab)]
    ss = (parts[0] if len(parts) == 1 else jnp.concatenate(parts, axis=1)) * (1.0 / 64.0)
    return x * lax.rsqrt(ss + EPS) * g


def _in_proj_kernel(h_ref, g_ref, wqa_ref, wsm_ref, wqi_ref, wqb_ref, wkb_ref, wvb_ref,
                    cos_ref, sin_ref, gsum_ref, gqa_ref, gka_ref, gqb_ref, gkb_ref,
                    qa_ref, ka_ref, va_ref, qi_ref, ki_ref, wi_ref, qb_ref, kb_ref, vb_ref):
    u = _rms_rows(h_ref[...], g_ref[...]).astype(BF16)
    cos_t, sin_t = cos_ref[...], sin_ref[...]
    gsum = gsum_ref[...]

    qa = _rope(_head_norm(_dot(u, wqa_ref[...]), gsum, gqa_ref[...]), cos_t, sin_t)
    qa_ref[...] = (qa * (D_A ** -0.5 * LOG2E)).astype(BF16)

    sm = _dot(u, wsm_ref[...])
    kv, kw = sm[:, 0:LANES], sm[:, LANES:2 * LANES]
    kv_swapped, kw_swapped = pltpu.roll(kv, D_A, 1), pltpu.roll(kw, D_I, 1)
    lo_half = lax.broadcasted_iota(jnp.int32, (sm.shape[0], LANES), 1) < D_A
    ka2 = jnp.where(lo_half, kv, kv_swapped)
    ka_ref[...] = _rope(_head_norm(ka2, gsum[0:LANES, 0:LANES], gka_ref[...]), cos_t, sin_t).astype(BF16)
    va_ref[...] = jnp.where(lo_half, kv_swapped, 1.0).astype(BF16)
    ki_ref[...] = _rope(jnp.where(lo_half, kw, kw_swapped), cos_t, sin_t).astype(BF16)

    qi_ref[...] = _rope(_dot(u, wqi_ref[...]), cos_t, sin_t).astype(BF16)
    wi_t = kw.T
    wi_ref[...] = wi_t[D_I:D_I + SUBLANES, :] * (H_I ** -0.5 * D_I ** -0.5)

    qb = _rope(_head_norm(_dot(u, wqb_ref[...]), gsum, gqb_ref[...]), cos_t, sin_t)
    qb_ref[...] = (qb * (D_B ** -0.5 * LOG2E)).astype(BF16)
    kb = _rope(_head_norm(_dot(u, wkb_ref[...]), gsum, gkb_ref[...]), cos_t, sin_t)
    kb_ref[...] = kb.astype(BF16)
    vb_ref[...] = _dot(u, wvb_ref[...]).astype(BF16)


def _in_proj(h, g, w, tabs, seq_len, tm):
    t, d = h.shape
    nseq = seq_len // tm

    def full(a):
        return pl.BlockSpec(a.shape, lambda i: (0,) * a.ndim)

    def rows(width):
        return pl.BlockSpec((tm, width), lambda i: (i, 0))

    tab = pl.BlockSpec((tm, LANES), lambda i: (i % nseq, 0))
    ins = [h, g, w["qa"], w["sm"], w["qi"], w["qb"], w["kb"], w["vb"],
           tabs["cos"], tabs["sin"], tabs["gsum"], w["gqa"], w["gka"], w["gqb"], w["gkb"]]
    in_specs = [rows(d), full(g)] + [full(a) for a in ins[2:8]] + [tab, tab] + [full(a) for a in ins[10:]]
    widths = [H_A * D_A, LANES, LANES, H_I * D_I, LANES, None, 2 * H_B * D_B, 2 * H_B * D_B, 2 * H_B * D_B]
    out_shape, out_specs = [], []
    for wd in widths:
        if wd is None:
            out_shape.append(jax.ShapeDtypeStruct((SUBLANES, t), F32))
            out_specs.append(pl.BlockSpec((SUBLANES, tm), lambda i: (0, i)))
        else:
            out_shape.append(jax.ShapeDtypeStruct((t, wd), BF16))
            out_specs.append(rows(wd))
    return pl.pallas_call(
        _in_proj_kernel, grid=(t // tm,), in_specs=in_specs, out_specs=out_specs, out_shape=out_shape,
        compiler_params=_cparams(1), name="in_proj")(*ins)


def _key_to_float(k):
    return lax.bitcast_convert_type(k ^ ((k >> 31) & 0x7FFFFFFF), F32)


def _dsa_kernel(qa_ref, qi_ref, wi_ref, ki_ref, ka_ref, va_ref, ltri_ref, o_ref,
                score_ref, coarse_ref, bias_ref, qs_ref, m_ref, acc_ref,
                *, tq, tk, k_top):
    j = pl.program_id(1)
    nkb = (j + 1) * (tq // tk)
    nh = H_A
    groups = tk // SUBLANES
    groups16 = tk // PACKED_ROWS
    no_limit = 2 ** 30
    per_trip = 2 if (tq // tk) % 2 == 0 else 1

    def for_blocks(block_fn, init):
        def trip(t, carry):
            for r in range(per_trip):
                carry = block_fn(t * per_trip + r, carry)
            return carry
        return lax.fori_loop(0, nkb // per_trip, trip, init)

    lane_q = lax.broadcasted_iota(jnp.int32, (tq, LANES), 1)
    lo_half = lane_q < 64

    qi = qi_ref[...]
    qi_heads = []
    for h in range(H_I):
        pair = qi[:, (h // 2) * LANES:(h // 2 + 1) * LANES]
        keep = lo_half if h % 2 == 0 else jnp.logical_not(lo_half)
        qi_heads.append(jnp.where(keep, pair, jnp.zeros_like(pair)))
    wi = wi_ref[...]
    t_idx = j * tq + lax.broadcasted_iota(jnp.int32, (tk, tq), 1)
    s_loc = lax.broadcasted_iota(jnp.int32, (tk, tq), 0)

    def score_body(kb, carry):
        kblk = ki_ref[pl.ds(pl.multiple_of(kb * tk, tk), tk), :]
        acc = jnp.zeros((tk, tq), F32)
        for h in range(H_I):
            acc = acc + wi[h:h + 1, :] * jnp.maximum(_nt_dot(kblk, qi_heads[h]), 0.0)
        sc = jnp.where(kb * tk + s_loc <= t_idx, acc, -jnp.inf)
        score_ref[kb] = sc.reshape(groups, SUBLANES, tq)
        coarse_ref[kb] = sc.astype(BF16).reshape(groups16, PACKED_ROWS, tq)
        return carry

    for_blocks(score_body, 0)

    ways = 4

    def count(pred):
        def body(kb, acc):
            hit = pred(score_ref[kb]).astype(jnp.int32)
            part = groups // ways
            sums = [jnp.sum(hit[w * part:(w + 1) * part], axis=0) for w in range(ways)]
            return acc + ((sums[0] + sums[1]) + (sums[2] + sums[3]))
        acc = for_blocks(body, jnp.zeros((SUBLANES, tq), jnp.int32))
        return jnp.sum(acc, axis=0, keepdims=True)

    def count_coarse(c16):
        one, zero = jnp.ones((), BF16), jnp.zeros((), BF16)

        def body(kb, acc):
            hit = jnp.where(coarse_ref[kb] >= c16, one, zero)
            part = groups16 // ways
            sums = [functools.reduce(lambda a, b: a + b, [hit[w * part + r] for r in range(part)])
                    for w in range(ways)]
            return acc + ((sums[0] + sums[1]) + (sums[2] + sums[3])).astype(F32)
        acc = for_blocks(body, jnp.zeros((PACKED_ROWS, tq), F32))
        return jnp.sum(acc, axis=0, keepdims=True)

    def bcast8(v):
        return jnp.broadcast_to(v, (SUBLANES, tq))[None]

    def coarse_body(step, thr16):
        cand = jnp.where(step == 0, 0, thr16 | jnp.left_shift(1, jnp.maximum(15 - step, 0)))
        bits = jnp.left_shift(cand ^ ((cand >> 15) & 0x7FFF), 16)
        c16 = lax.bitcast_convert_type(bits, F32).astype(BF16)
        n_ge = count_coarse(jnp.broadcast_to(c16, (PACKED_ROWS, tq))[None])
        return jnp.where(n_ge >= k_top, cand, thr16)

    thr16 = lax.fori_loop(0, 16, coarse_body, jnp.full((1, tq), -(2 ** 15), jnp.int32))
    base = jnp.left_shift(thr16, 16)
    lo_key = jnp.where(base < INT_MIN + 2 ** 16, INT_MIN, base - 2 ** 16)

    def fine_body(step, thr_key):
        cand = thr_key + jnp.left_shift(1, 17 - step)
        cand = jnp.where((thr_key > 0) & (cand < 0), INT_MAX, cand)
        c8 = bcast8(_key_to_float(cand))
        n_ge = count(lambda blk: blk >= c8)
        return jnp.where(n_ge >= k_top, cand, thr_key)

    thr_key = lax.fori_loop(0, 18, fine_body, lo_key)
    n_valid = j * tq + lax.broadcasted_iota(jnp.int32, (1, tq), 1) + 1
    take_all = n_valid <= k_top
    thr = jnp.where(take_all, -jnp.inf, _key_to_float(thr_key))
    t8 = bcast8(thr)
    n_gt = count(lambda blk: blk > t8)
    need = jnp.where(take_all, no_limit, k_top - n_gt).astype(F32)

    def bias_body(kb, ties_before):
        sc = score_ref[kb].reshape(tk, tq)
        eq = sc == thr
        incl = _dot(ltri_ref[...], jnp.where(eq, 1.0, 0.0).astype(BF16))
        sel = (sc > thr) | (eq & (incl + ties_before <= need))
        sel = sel & (kb * tk + s_loc <= t_idx)
        bias_ref[kb] = jnp.where(sel, 0.0, NEG_MASKED).astype(F32).T
        return ties_before + incl[tk - 1:tk, :]

    for_blocks(bias_body, jnp.zeros((1, tq), F32))

    qa = qa_ref[...]
    for h in range(nh):
        pair = qa[:, (h // 2) * LANES:(h // 2 + 1) * LANES]
        keep = lo_half if h % 2 == 0 else jnp.logical_not(lo_half)
        qs_ref[h * tq:(h + 1) * tq, :] = jnp.where(keep, pair, jnp.zeros_like(pair))
    m_ref[...] = jnp.full(m_ref.shape, NEG_INIT, F32)
    acc_ref[...] = jnp.zeros(acc_ref.shape, F32)

    def attn_body(kb, carry):
        k0 = pl.multiple_of(kb * tk, tk)
        kblk = ka_ref[pl.ds(k0, tk), :]
        vblk = va_ref[pl.ds(k0, tk), :]
        for h in range(nh):
            rows = slice(h * tq, (h + 1) * tq)
            s = _nt_dot(qs_ref[rows, :], kblk) + bias_ref[kb]
            m_prev = m_ref[rows, :]
            m_new = jnp.maximum(m_prev, jnp.max(s, axis=1, keepdims=True))
            p = jnp.exp2(s - jnp.tile(m_new, (1, tk // LANES)))
            acc_ref[rows, :] = acc_ref[rows, :] * jnp.exp2(m_prev - m_new) + _dot(p.astype(BF16), vblk)
            m_ref[rows, :] = m_new
        return carry

    for_blocks(attn_body, 0)

    for pr in range(nh // 2):
        even = acc_ref[2 * pr * tq:(2 * pr + 1) * tq, :]
        odd = acc_ref[(2 * pr + 1) * tq:(2 * pr + 2) * tq, :]
        num = jnp.where(lo_half, even, pltpu.roll(odd, 64, 1))
        den = jnp.where(lo_half, pltpu.roll(even, 64, 1), odd)
        o_ref[:, pr * LANES:(pr + 1) * LANES] = (num / den).astype(BF16)


def _dsa(qa, qi, wi_t, ki, ka, va, tq, tk):
    b, l, _ = qa.shape
    nq = l // tq
    k_top = min(TOPK_MAX, l // 4)
    kern = functools.partial(_dsa_kernel, tq=tq, tk=tk, k_top=k_top)
    per_q = lambda w: pl.BlockSpec((None, tq, w), lambda bi, j: (bi, j, 0))
    per_b = pl.BlockSpec((None, l, LANES), lambda bi, j: (bi, 0, 0))
    ltri = jnp.tril(jnp.ones((tk, tk), F32)).astype(BF16)
    return pl.pallas_call(
        kern, grid=(b, nq),
        in_specs=[per_q(H_A * D_A), per_q(H_I * D_I),
                  pl.BlockSpec((SUBLANES, tq), lambda bi, j: (0, bi * nq + j)),
                  per_b, per_b, per_b,
                  pl.BlockSpec((tk, tk), lambda bi, j: (0, 0))],
        out_specs=per_q(H_A * D_A),
        out_shape=jax.ShapeDtypeStruct((b, l, H_A * D_A), BF16),
        scratch_shapes=[
            pltpu.VMEM((l // tk, tk // SUBLANES, SUBLANES, tq), F32),
            pltpu.VMEM((l // tk, tk // PACKED_ROWS, PACKED_ROWS, tq), BF16),
            pltpu.VMEM((l // tk, tq, tk), F32),
            pltpu.VMEM((H_A * tq, LANES), BF16),
            pltpu.VMEM((H_A * tq, LANES), F32),
            pltpu.VMEM((H_A * tq, LANES), F32),
        ],
        compiler_params=_cparams(2), name="dsa")(qa, qi, wi_t, ki, ka, va, ltri)


def _diff_kernel(qb_ref, kb_ref, vb_ref, lam_ref, gsub_ref, o_ref, qs_ref, m_ref, l_ref, acc_ref,
                 *, tq, tk, lam_init):
    j = pl.program_id(1)
    diag_blocks = tq // tk
    lamv = lam_ref[...]
    lam = (jnp.exp(jnp.sum(lamv[0:1] * lamv[1:2], axis=1, keepdims=True))
           - jnp.exp(jnp.sum(lamv[2:3] * lamv[3:4], axis=1, keepdims=True)) + lam_init)
    lane_q = lax.broadcasted_iota(jnp.int32, (tq, LANES), 1)
    lo_half = lane_q < 64
    row = lax.broadcasted_iota(jnp.int32, (2 * tq, tk), 0)
    t_loc = jnp.where(row >= tq, row - tq, row)
    s_loc = lax.broadcasted_iota(jnp.int32, (2 * tq, tk), 1)

    for h in range(H_B):
        pair = qb_ref[:, h * LANES:(h + 1) * LANES]
        qs_ref[h, 0:tq, :] = jnp.where(lo_half, pair, jnp.zeros_like(pair))
        qs_ref[h, tq:2 * tq, :] = jnp.where(lo_half, jnp.zeros_like(pair), pair)
    m_ref[...] = jnp.full(m_ref.shape, NEG_INIT, F32)
    l_ref[...] = jnp.zeros(l_ref.shape, F32)
    acc_ref[...] = jnp.zeros(acc_ref.shape, F32)

    def step(kb, diag):
        k0 = pl.multiple_of(kb * tk, tk)
        for h in range(H_B):
            cols = slice(h * LANES, (h + 1) * LANES)
            s = _nt_dot(qs_ref[h], kb_ref[pl.ds(k0, tk), cols])
            if diag is not None:
                s = jnp.where(diag * tk + s_loc <= t_loc, s, NEG_MASKED)
            m_prev = m_ref[h]
            m_new = jnp.maximum(m_prev, jnp.max(s, axis=1, keepdims=True))
            alpha = jnp.exp2(m_prev - m_new)
            p = jnp.exp2(s - jnp.tile(m_new, (1, tk // LANES)))
            l_ref[h] = alpha * l_ref[h] + jnp.sum(p, axis=1, keepdims=True)
            acc_ref[h] = alpha * acc_ref[h] + _dot(p.astype(BF16), vb_ref[pl.ds(k0, tk), cols])
            m_ref[h] = m_new

    def body(kb, carry):
        step(kb, None)
        return carry

    lax.fori_loop(0, j * diag_blocks, body, 0)
    for dblk in range(diag_blocks):
        step(j * diag_blocks + dblk, dblk)

    for h in range(H_B):
        o = acc_ref[h] / l_ref[h]
        o = o[0:tq] - lam * o[tq:2 * tq]
        y = _rms_rows(o, gsub_ref[...]) * (1.0 - lam_init)
        o_ref[:, h * LANES:(h + 1) * LANES] = y.astype(BF16)


def _diffattn(qb, kb, vb, lam_p, gsub, tq, tk, lam_init):
    b, l, w = qb.shape
    nq = l // tq
    kern = functools.partial(_diff_kernel, tq=tq, tk=tk, lam_init=lam_init)
    per_q = pl.BlockSpec((None, tq, w), lambda bi, j: (bi, j, 0))
    per_b = pl.BlockSpec((None, l, w), lambda bi, j: (bi, 0, 0))
    full = lambda a: pl.BlockSpec(a.shape, lambda bi, j: (0,) * a.ndim)
    return pl.pallas_call(
        kern, grid=(b, nq),
        in_specs=[per_q, per_b, per_b, full(lam_p), full(gsub)],
        out_specs=per_q,
        out_shape=jax.ShapeDtypeStruct((b, l, w), BF16),
        scratch_shapes=[pltpu.VMEM((H_B, 2 * tq, LANES), BF16),
                        pltpu.VMEM((H_B, 2 * tq, LANES), F32),
                        pltpu.VMEM((H_B, 2 * tq, LANES), F32),
                        pltpu.VMEM((H_B, 2 * tq, LANES), F32)],
        compiler_params=_cparams(2), name="diffattn")(qb, kb, vb, lam_p, gsub)


def _merge_kernel(h_ref, g_ref, oa_ref, ob_ref, wg_ref, wa_ref, wb_ref, wo_ref, out_ref):
    h = h_ref[...]
    d = h.shape[-1]
    u = _rms_rows(h, g_ref[...]).astype(BF16)
    gates = jax.nn.sigmoid(_dot(u, wg_ref[...]))
    mix = (gates[:, 0:d] * _dot(oa_ref[...], wa_ref[...])
           + gates[:, d:2 * d] * _dot(ob_ref[...], wb_ref[...]))
    out_ref[...] = h + _dot(mix.astype(BF16), wo_ref[...])


def _layer_block(a, layer, **kw):
    return pl.BlockSpec((None,) + a.shape[1:], lambda i: (layer,) + (0,) * (a.ndim - 1), **kw)


def _merge(h, g, oa, ob, wg, wa, wb, wo, layer, tm):
    t, d = h.shape
    rows = lambda w: pl.BlockSpec((tm, w), lambda i: (i, 0))
    full = lambda a: pl.BlockSpec(a.shape, lambda i: (0,) * a.ndim)
    of_layer = lambda a: _layer_block(a, layer)
    return pl.pallas_call(
        _merge_kernel, grid=(t // tm,),
        in_specs=[rows(d), full(g), rows(oa.shape[1]), rows(ob.shape[1]), full(wg),
                  of_layer(wa), of_layer(wb), of_layer(wo)],
        out_specs=rows(d), out_shape=jax.ShapeDtypeStruct((t, d), F32),
        compiler_params=_cparams(1), name="merge")(h, g, oa, ob, wg, wa, wb, wo)


def _gelu_tanh(x):
    return 0.5 * x * (1.0 + jnp.tanh(math.sqrt(2.0 / math.pi) * (x + 0.044715 * (x * x * x))))


def _ffn_ple_kernel(h_ref, hn_ref, g_ref, wu_ref, cp_ref, wd_ref, p_ref, gp_ref, wpg_ref, wpp_ref, out_ref,
                    tail_ref, xa_ref, xb_ref, xn_ref, acc_ref, *, tiles_per_seq, cw):
    i = pl.program_id(0)
    tm = h_ref.shape[0]
    d_ff = wd_ref.shape[0]
    nc = d_ff // cw
    npan = cw // LANES
    h = h_ref[...]
    acc_ref[...] = jnp.zeros(acc_ref.shape, F32)

    def up_proj(u, c, x_ref, seq_start):
        for half, base in enumerate((0, d_ff)):
            cols = pl.ds(pl.multiple_of(base + c * cw, LANES), cw)
            x = _dot(u, wu_ref[:, cols])
            prev = jnp.where(seq_start, 0.0, tail_ref[:, cols])
            tail_ref[:, cols] = x[tm - SUBLANES:tm, :]
            for p in range(npan):
                x_ref[half * npan + p, 0:SUBLANES, :] = prev[:, p * LANES:(p + 1) * LANES]
                x_ref[half * npan + p, SUBLANES:SUBLANES + tm, :] = x[:, p * LANES:(p + 1) * LANES]

    def gated(c, x_ref):
        outs = []
        for p in range(npan):
            conv = []
            for half, base in enumerate((0, d_ff)):
                taps = cp_ref[:, pl.ds(pl.multiple_of(base + c * cw + p * LANES, LANES), LANES)]
                q = half * npan + p
                conv.append(taps[3:4, :]
                            + x_ref[q, SUBLANES - 2:SUBLANES - 2 + tm, :] * taps[0:1, :]
                            + x_ref[q, SUBLANES - 1:SUBLANES - 1 + tm, :] * taps[1:2, :]
                            + x_ref[q, SUBLANES:SUBLANES + tm, :] * taps[2:3, :])
            outs.append((_gelu_tanh(conv[0]) * conv[1]).astype(BF16))
        return jnp.concatenate(outs, axis=1)

    def down(c, act):
        acc_ref[...] += _dot(act, wd_ref[pl.ds(pl.multiple_of(c * cw, cw), cw), :])

    @pl.when(i == 0)
    def _():
        up_proj(_rms_rows(h, g_ref[...]).astype(BF16), 0, xn_ref, True)

    u = _rms_rows(h, g_ref[...]).astype(BF16)
    seq_start = (i % tiles_per_seq) == 0
    up_proj(u, 1, xa_ref, seq_start)
    down(0, gated(0, xn_ref))
    up_proj(u, 2, xb_ref, seq_start)
    down(1, gated(1, xa_ref))

    pairs = (nc - 3) // 2
    pairs_per_trip = 2 if pairs % 2 == 0 else 1

    def trip(t, carry):
        for r in range(pairs_per_trip):
            c = 2 * (t * pairs_per_trip + r) + 2
            up_proj(u, c + 1, xa_ref, seq_start)
            down(c, gated(c, xb_ref))
            up_proj(u, c + 2, xb_ref, seq_start)
            down(c + 1, gated(c + 1, xa_ref))
        return carry

    lax.fori_loop(0, pairs // pairs_per_trip, trip, 0)
    up_proj(_rms_rows(hn_ref[...], g_ref[...]).astype(BF16), 0, xn_ref, ((i + 1) % tiles_per_seq) == 0)
    down(nc - 1, gated(nc - 1, xb_ref))
    h2 = h + acc_ref[...]
    gate = jax.nn.sigmoid(_dot(_rms_rows(h2, gp_ref[...]).astype(BF16), wpg_ref[...]))
    out_ref[...] = h2 + gate * _dot(p_ref[...].astype(BF16), wpp_ref[...])


def _convffn_ple(h, g, wu, cp, wd, p, gp, wpg, wpp, layer, seq_len, tm, cw):
    t, d = h.shape
    n = t // tm
    nc = wd.shape[1] // cw
    assert nc % 2 == 1 and nc >= 3, "the chunk pipeline is written for an odd chunk count"
    rows = lambda w: pl.BlockSpec((tm, w), lambda i: (i, 0))
    next_rows = pl.BlockSpec((tm, d), lambda i: (jnp.minimum(i + 1, n - 1), 0))
    const = lambda a: pl.BlockSpec(a.shape, lambda i: (0,) * a.ndim, pipeline_mode=pl.Buffered(1))
    of_layer = lambda a: _layer_block(a, layer, pipeline_mode=pl.Buffered(1))
    p_rows = pl.BlockSpec((None, tm, p.shape[2]), lambda i: (layer, i, 0))
    kern = functools.partial(_ffn_ple_kernel, tiles_per_seq=seq_len // tm, cw=cw)
    panels = pltpu.VMEM((2 * cw // LANES, SUBLANES + tm, LANES), F32)
    return pl.pallas_call(
        kern, grid=(n,),
        in_specs=[rows(d), next_rows, const(g), of_layer(wu), const(cp), of_layer(wd),
                  p_rows, const(gp), of_layer(wpg), of_layer(wpp)],
        out_specs=rows(d), out_shape=jax.ShapeDtypeStruct((t, d), F32),
        scratch_shapes=[pltpu.VMEM((SUBLANES, wu.shape[2]), F32), panels, panels, panels,
                        pltpu.VMEM((tm, d), F32)],
        compiler_params=_cparams(1), name="convffn_ple")(h, h, g, wu, cp, wd, p, gp, wpg, wpp)


def _rope_tables(length):
    inv = 1.0 / (ROPE_THETA ** (jnp.arange(0, D_A, 2, dtype=F32) / D_A))
    ang = jnp.arange(length, dtype=F32)[:, None] * inv[None, :]
    c, s = jnp.cos(ang), jnp.sin(ang)
    return jnp.tile(c, (1, 4)), jnp.tile(jnp.concatenate([-s, s], axis=1), (1, 2))


def _pick_tile(n, pref):
    tile = min(n, pref)
    assert n % tile == 0, (n, tile)
    return tile


def kernel(x, p, g_mix_norm, w_in, g_qa, g_ka, g_qb, g_kb, lam_q1, lam_k1, lam_q2, lam_k2, g_subln,
           w_branch_a, w_branch_b, w_out, g_ffn_norm, w_up, conv_w, conv_b, w_down, g_ple_norm,
           w_ple_gate, w_ple_proj):
    bsz, seq_len, d_model = x.shape
    depth = w_in.shape[0]
    d_ff = w_down.shape[1]
    t = bsz * seq_len
    tq = _pick_tile(seq_len, 512)
    tk = _pick_tile(tq, 256)
    tk_diff = _pick_tile(tq, 512)
    tm = _pick_tile(seq_len, 512)
    tm_proj = _pick_tile(seq_len, 1024)
    cw = 256
    assert d_ff % cw == 0

    cos_t, sin_t = _rope_tables(seq_len)
    gsum = jnp.kron(jnp.eye(4, dtype=F32), jnp.ones((64, 64), F32)).astype(BF16)
    tabs = {"cos": cos_t, "sin": sin_t, "gsum": gsum}

    sizes = [H_A * D_A, D_A, D_A, H_I * D_I, D_I, H_I, 2 * H_B * D_B, 2 * H_B * D_B, 2 * H_B * D_B, d_model, d_model]
    offs = [0]
    for sz in sizes:
        offs.append(offs[-1] + sz)
    assert offs[-1] == w_in.shape[2]

    bf = lambda a: a.astype(BF16)
    w_a, w_b, w_o, w_u, w_d = bf(w_branch_a), bf(w_branch_b), bf(w_out), bf(w_up), bf(w_down)
    w_pg, w_pp = bf(w_ple_gate), bf(w_ple_proj)
    p_tok = p.reshape(depth, t, p.shape[-1])

    h = x.reshape(t, d_model)
    for i in range(depth):
        lam_init = 0.8 - 0.6 * math.exp(-0.3 * i)
        segs = lax.optimization_barrier([w_in[i, :, offs[k]:offs[k + 1]] for k in range(len(sizes))])
        seg = lambda k: segs[k]
        wts = {
            "qa": seg(0).astype(BF16),
            "sm": jnp.concatenate([seg(1), seg(2), seg(4),
                                   jnp.pad(seg(5), ((0, 0), (0, D_I - H_I)))], axis=1).astype(BF16),
            "qi": seg(3).astype(BF16),
            "qb": seg(6).astype(BF16), "kb": seg(7).astype(BF16), "vb": seg(8).astype(BF16),
            "gqa": jnp.tile(g_qa[i], H_A)[None, :], "gka": jnp.tile(g_ka[i], 2)[None, :],
            "gqb": jnp.tile(g_qb[i], 2 * H_B)[None, :], "gkb": jnp.tile(g_kb[i], 2 * H_B)[None, :],
        }
        qa, ka, va, qi, ki, wi_t, qb, kb, vb = _in_proj(h, g_mix_norm[i][None, :], wts, tabs, seq_len, tm_proj)
        r3 = lambda a: a.reshape(bsz, seq_len, a.shape[-1])
        o_a = _dsa(r3(qa), r3(qi), wi_t, r3(ki), r3(ka), r3(va), tq, tk)
        lam_p = jnp.stack([lam_q1[i], lam_k1[i], lam_q2[i], lam_k2[i]])
        o_b = _diffattn(r3(qb), r3(kb), r3(vb), lam_p, g_subln[i][None, :], tq, tk_diff, lam_init)

        w_gate = jnp.concatenate([seg(9), seg(10)], axis=1).astype(BF16)
        h = _merge(h, g_mix_norm[i][None, :], o_a.reshape(t, -1), o_b.reshape(t, -1), w_gate,
                   w_a, w_b, w_o, i, tm_proj)

        conv = jnp.concatenate([conv_w[i], conv_b[i][None, :]], axis=0)
        conv = jnp.pad(conv, ((0, SUBLANES - CONV_W - 1), (0, 0)))
        h = _convffn_ple(h, g_ffn_norm[i][None, :], w_u, conv, w_d, p_tok, g_ple_norm[i][None, :], w_pg, w_pp,
                         i, seq_len, tm, cw)
    return h.reshape(bsz, seq_len, d_model)
```

```python
import functools
import math

import jax
import jax.numpy as jnp
from jax import lax
from jax.experimental import pallas as pl
from jax.experimental.pallas import tpu as pltpu

H_A, D_A = 8, 64
H_I, D_I = 4, 64
H_B, D_B = 4, 64
TOPK_MAX = 256
CONV_W = 3
ROPE_THETA = 10000.0
EPS = 1e-6

LANES = 128
SUBLANES = 8
PACKED_ROWS = 16
VMEM_LIMIT = 56 * 1024 * 1024

F32 = jnp.float32
BF16 = jnp.bfloat16
NEG_MASKED = -2e30
NEG_INIT = -1e30
INT_MIN = -(2 ** 31)
INT_MAX = 2 ** 31 - 1
LOG2E = math.log2(math.e)


def _nt_dot(a, b):
    return lax.dot_general(a, b, (((1,), (1,)), ((), ())), preferred_element_type=F32)


def _dot(a, b):
    return jnp.dot(a, b, preferred_element_type=F32)


def _cparams(n_axes):
    return pltpu.CompilerParams(dimension_semantics=("arbitrary",) * n_axes,
                                vmem_limit_bytes=VMEM_LIMIT)


def _rms_rows(x, g):
    ms = jnp.mean(x * x, axis=-1, keepdims=True)
    return x * lax.rsqrt(ms + EPS) * g


def _rope(x, cos_t, sin_t):
    w = x.shape[-1]
    reps = w // LANES
    c = jnp.tile(cos_t, (1, reps)) if reps > 1 else cos_t
    s = jnp.tile(sin_t, (1, reps)) if reps > 1 else sin_t
    up = pltpu.roll(x, w - 32, 1)
    dn = pltpu.roll(x, 32, 1)
    lane = lax.broadcasted_iota(jnp.int32, x.shape, 1)
    partner = jnp.where((lane & 63) < 32, up, dn)
    return x * c + partner * s


def _head_norm(x, gsum, g):
    slab = gsum.shape[0]
    sq = (x * x).astype(BF16)
    parts = [_dot(sq[:, c:c + slab], gsum) for c in range(0, x.shape[-1], slab)]
    ss = (parts[0] if len(parts) == 1 else jnp.concatenate(parts, axis=1)) * (1.0 / 64.0)
    return x * lax.rsqrt(ss + EPS) * g


def _in_proj_kernel(h_ref, g_ref, wqa_ref, wsm_ref, wqi_ref, wqb_ref, wkb_ref, wvb_ref,
                    cos_ref, sin_ref, gsum_ref, gqa_ref, gka_ref, gqb_ref, gkb_ref,
                    qa_ref, ka_ref, va_ref, qi_ref, ki_ref, wi_ref, qb_ref, kb_ref, vb_ref):
    u = _rms_rows(h_ref[...], g_ref[...]).astype(BF16)
    cos_t, sin_t = cos_ref[...], sin_ref[...]
    gsum = gsum_ref[...]

    qa = _rope(_head_norm(_dot(u, wqa_ref[...]), gsum, gqa_ref[...]), cos_t, sin_t)
    qa_ref[...] = (qa * (D_A ** -0.5 * LOG2E)).astype(BF16)

    sm = _dot(u, wsm_ref[...])
    kv, kw = sm[:, 0:LANES], sm[:, LANES:2 * LANES]
    kv_swapped, kw_swapped = pltpu.roll(kv, D_A, 1), pltpu.roll(kw, D_I, 1)
    lo_half = lax.broadcasted_iota(jnp.int32, (sm.shape[0], LANES), 1) < D_A
    ka2 = jnp.where(lo_half, kv, kv_swapped)
    ka_ref[...] = _rope(_head_norm(ka2, gsum[0:LANES, 0:LANES], gka_ref[...]), cos_t, sin_t).astype(BF16)
    va_ref[...] = jnp.where(lo_half, kv_swapped, 1.0).astype(BF16)
    ki_ref[...] = _rope(jnp.where(lo_half, kw, kw_swapped), cos_t, sin_t).astype(BF16)

    qi_ref[...] = _rope(_dot(u, wqi_ref[...]), cos_t, sin_t).astype(BF16)
    wi_t = kw.T
    wi_ref[...] = wi_t[D_I:D_I + SUBLANES, :] * (H_I ** -0.5 * D_I ** -0.5)

    qb = _rope(_head_norm(_dot(u, wqb_ref[...]), gsum, gqb_ref[...]), cos_t, sin_t)
    qb_ref[...] = (qb * (D_B ** -0.5 * LOG2E)).astype(BF16)
    kb = _rope(_head_norm(_dot(u, wkb_ref[...]), gsum, gkb_ref[...]), cos_t, sin_t)
    kb_ref[...] = kb.astype(BF16)
    vb_ref[...] = _dot(u, wvb_ref[...]).astype(BF16)


def _in_proj(h, g, w, tabs, seq_len, tm):
    t, d = h.shape
    nseq = seq_len // tm

    def full(a):
        return pl.BlockSpec(a.shape, lambda i: (0,) * a.ndim)

    def rows(width):
        return pl.BlockSpec((tm, width), lambda i: (i, 0))

    tab = pl.BlockSpec((tm, LANES), lambda i: (i % nseq, 0))
    ins = [h, g, w["qa"], w["sm"], w["qi"], w["qb"], w["kb"], w["vb"],
           tabs["cos"], tabs["sin"], tabs["gsum"], w["gqa"], w["gka"], w["gqb"], w["gkb"]]
    in_specs = [rows(d), full(g)] + [full(a) for a in ins[2:8]] + [tab, tab] + [full(a) for a in ins[10:]]
    widths = [H_A * D_A, LANES, LANES, H_I * D_I, LANES, None, 2 * H_B * D_B, 2 * H_B * D_B, 2 * H_B * D_B]
    out_shape, out_specs = [], []
    for wd in widths:
        if wd is None:
            out_shape.append(jax.ShapeDtypeStruct((SUBLANES, t), F32))
            out_specs.append(pl.BlockSpec((SUBLANES, tm), lambda i: (0, i)))
        else:
            out_shape.append(jax.ShapeDtypeStruct((t, wd), BF16))
            out_specs.append(rows(wd))
    return pl.pallas_call(
        _in_proj_kernel, grid=(t // tm,), in_specs=in_specs, out_specs=out_specs, out_shape=out_shape,
        compiler_params=_cparams(1), name="in_proj")(*ins)


def _key_to_float(k):
    return lax.bitcast_convert_type(k ^ ((k >> 31) & 0x7FFFFFFF), F32)


def _dsa_kernel(qa_ref, qi_ref, wi_ref, ki_ref, ka_ref, va_ref, ltri_ref, o_ref,
                score_ref, coarse_ref, bias_ref, qs_ref, m_ref, acc_ref,
                *, tq, tk, k_top):
    j = pl.program_id(1)
    nkb = (j + 1) * (tq // tk)
    nh = H_A
    groups = tk // SUBLANES
    groups16 = tk // PACKED_ROWS
    no_limit = 2 ** 30
    per_trip = 2 if (tq // tk) % 2 == 0 else 1

    def for_blocks(block_fn, init):
        def trip(t, carry):
            for r in range(per_trip):
                carry = block_fn(t * per_trip + r, carry)
            return carry
        return lax.fori_loop(0, nkb // per_trip, trip, init)

    lane_q = lax.broadcasted_iota(jnp.int32, (tq, LANES), 1)
    lo_half = lane_q < 64

    qi = qi_ref[...]
    qi_heads = []
    for h in range(H_I):
        pair = qi[:, (h // 2) * LANES:(h // 2 + 1) * LANES]
        keep = lo_half if h % 2 == 0 else jnp.logical_not(lo_half)
        qi_heads.append(jnp.where(keep, pair, jnp.zeros_like(pair)))
    wi = wi_ref[...]
    t_idx = j * tq + lax.broadcasted_iota(jnp.int32, (tk, tq), 1)
    s_loc = lax.broadcasted_iota(jnp.int32, (tk, tq), 0)

    def score_body(kb, carry):
        kblk = ki_ref[pl.ds(pl.multiple_of(kb * tk, tk), tk), :]
        acc = jnp.zeros((tk, tq), F32)
        for h in range(H_I):
            acc = acc + wi[h:h + 1, :] * jnp.maximum(_nt_dot(kblk, qi_heads[h]), 0.0)
        sc = jnp.where(kb * tk + s_loc <= t_idx, acc, -jnp.inf)
        score_ref[kb] = sc.reshape(groups, SUBLANES, tq)
        coarse_ref[kb] = sc.astype(BF16).reshape(groups16, PACKED_ROWS, tq)
        return carry

    for_blocks(score_body, 0)

    ways = 4

    def count(pred):
        def body(kb, acc):
            hit = pred(score_ref[kb]).astype(jnp.int32)
            part = groups // ways
            sums = [jnp.sum(hit[w * part:(w + 1) * part], axis=0) for w in range(ways)]
            return acc + ((sums[0] + sums[1]) + (sums[2] + sums[3]))
        acc = for_blocks(body, jnp.zeros((SUBLANES, tq), jnp.int32))
        return jnp.sum(acc, axis=0, keepdims=True)

    def count_coarse(c16):
        one, zero = jnp.ones((), BF16), jnp.zeros((), BF16)

        def body(kb, acc):
            hit = jnp.where(coarse_ref[kb] >= c16, one, zero)
            part = groups16 // ways
            sums = [functools.reduce(lambda a, b: a + b, [hit[w * part + r] for r in range(part)])
                    for w in range(ways)]
            return acc + ((sums[0] + sums[1]) + (sums[2] + sums[3])).astype(F32)
        acc = for_blocks(body, jnp.zeros((PACKED_ROWS, tq), F32))
        return jnp.sum(acc, axis=0, keepdims=True)

    def bcast8(v):
        return jnp.broadcast_to(v, (SUBLANES, tq))[None]

    def coarse_body(step, thr16):
        cand = jnp.where(step == 0, 0, thr16 | jnp.left_shift(1, jnp.maximum(15 - step, 0)))
        bits = jnp.left_shift(cand ^ ((cand >> 15) & 0x7FFF), 16)
        c16 = lax.bitcast_convert_type(bits, F32).astype(BF16)
        n_ge = count_coarse(jnp.broadcast_to(c16, (PACKED_ROWS, tq))[None])
        return jnp.where(n_ge >= k_top, cand, thr16)

    thr16 = lax.fori_loop(0, 16, coarse_body, jnp.full((1, tq), -(2 ** 15), jnp.int32))
    base = jnp.left_shift(thr16, 16)
    lo_key = jnp.where(base < INT_MIN + 2 ** 16, INT_MIN, base - 2 ** 16)

    def fine_body(step, thr_key):
        cand = thr_key + jnp.left_shift(1, 17 - step)
        cand = jnp.where((thr_key > 0) & (cand < 0), INT_MAX, cand)
        c8 = bcast8(_key_to_float(cand))
        n_ge = count(lambda blk: blk >= c8)
        return jnp.where(n_ge >= k_top, cand, thr_key)

    thr_key = lax.fori_loop(0, 18, fine_body, lo_key)
    n_valid = j * tq + lax.broadcasted_iota(jnp.int32, (1, tq), 1) + 1
    take_all = n_valid <= k_top
    thr = jnp.where(take_all, -jnp.inf, _key_to_float(thr_key))
    t8 = bcast8(thr)
    n_gt = count(lambda blk: blk > t8)
    need = jnp.where(take_all, no_limit, k_top - n_gt).astype(F32)

    def bias_body(kb, ties_before):
        sc = score_ref[kb].reshape(tk, tq)
        eq = sc == thr
        incl = _dot(ltri_ref[...], jnp.where(eq, 1.0, 0.0).astype(BF16))
        sel = (sc > thr) | (eq & (incl + ties_before <= need))
        sel = sel & (kb * tk + s_loc <= t_idx)
        bias_ref[kb] = jnp.where(sel, 0.0, NEG_MASKED).astype(F32).T
        return ties_before + incl[tk - 1:tk, :]

    for_blocks(bias_body, jnp.zeros((1, tq), F32))

    qa = qa_ref[...]
    for h in range(nh):
        pair = qa[:, (h // 2) * LANES:(h // 2 + 1) * LANES]
        keep = lo_half if h % 2 == 0 else jnp.logical_not(lo_half)
        qs_ref[h * tq:(h + 1) * tq, :] = jnp.where(keep, pair, jnp.zeros_like(pair))
    m_ref[...] = jnp.full(m_ref.shape, NEG_INIT, F32)
    acc_ref[...] = jnp.zeros(acc_ref.shape, F32)

    def attn_body(kb, carry):
        k0 = pl.multiple_of(kb * tk, tk)
        kblk = ka_ref[pl.ds(k0, tk), :]
        vblk = va_ref[pl.ds(k0, tk), :]
        for h in range(nh):
            rows = slice(h * tq, (h + 1) * tq)
            s = _nt_dot(qs_ref[rows, :], kblk) + bias_ref[kb]
            m_prev = m_ref[rows, :]
            m_new = jnp.maximum(m_prev, jnp.max(s, axis=1, keepdims=True))
            p = jnp.exp2(s - jnp.tile(m_new, (1, tk // LANES)))
            acc_ref[rows, :] = acc_ref[rows, :] * jnp.exp2(m_prev - m_new) + _dot(p.astype(BF16), vblk)
            m_ref[rows, :] = m_new
        return carry

    for_blocks(attn_body, 0)

    for pr in range(nh // 2):
        even = acc_ref[2 * pr * tq:(2 * pr + 1) * tq, :]
        odd = acc_ref[(2 * pr + 1) * tq:(2 * pr + 2) * tq, :]
        num = jnp.where(lo_half, even, pltpu.roll(odd, 64, 1))
        den = jnp.where(lo_half, pltpu.roll(even, 64, 1), odd)
        o_ref[:, pr * LANES:(pr + 1) * LANES] = (num / den).astype(BF16)


def _dsa(qa, qi, wi_t, ki, ka, va, tq, tk):
    b, l, _ = qa.shape
    nq = l // tq
    k_top = min(TOPK_MAX, l // 4)
    kern = functools.partial(_dsa_kernel, tq=tq, tk=tk, k_top=k_top)
    per_q = lambda w: pl.BlockSpec((None, tq, w), lambda bi, j: (bi, j, 0))
    per_b = pl.BlockSpec((None, l, LANES), lambda bi, j: (bi, 0, 0))
    ltri = jnp.tril(jnp.ones((tk, tk), F32)).astype(BF16)
    return pl.pallas_call(
        kern, grid=(b, nq),
        in_specs=[per_q(H_A * D_A), per_q(H_I * D_I),
                  pl.BlockSpec((SUBLANES, tq), lambda bi, j: (0, bi * nq + j)),
                  per_b, per_b, per_b,
                  pl.BlockSpec((tk, tk), lambda bi, j: (0, 0))],
        out_specs=per_q(H_A * D_A),
        out_shape=jax.ShapeDtypeStruct((b, l, H_A * D_A), BF16),
        scratch_shapes=[
            pltpu.VMEM((l // tk, tk // SUBLANES, SUBLANES, tq), F32),
            pltpu.VMEM((l // tk, tk // PACKED_ROWS, PACKED_ROWS, tq), BF16),
            pltpu.VMEM((l // tk, tq, tk), F32),
            pltpu.VMEM((H_A * tq, LANES), BF16),
            pltpu.VMEM((H_A * tq, LANES), F32),
            pltpu.VMEM((H_A * tq, LANES), F32),
        ],
        compiler_params=_cparams(2), name="dsa")(qa, qi, wi_t, ki, ka, va, ltri)


def _diff_kernel(qb_ref, kb_ref, vb_ref, lam_ref, gsub_ref, o_ref, qs_ref, m_ref, l_ref, acc_ref,
                 *, tq, tk, lam_init):
    j = pl.program_id(1)
    diag_blocks = tq // tk
    lamv = lam_ref[...]
    lam = (jnp.exp(jnp.sum(lamv[0:1] * lamv[1:2], axis=1, keepdims=True))
           - jnp.exp(jnp.sum(lamv[2:3] * lamv[3:4], axis=1, keepdims=True)) + lam_init)
    lane_q = lax.broadcasted_iota(jnp.int32, (tq, LANES), 1)
    lo_half = lane_q < 64
    row = lax.broadcasted_iota(jnp.int32, (2 * tq, tk), 0)
    t_loc = jnp.where(row >= tq, row - tq, row)
    s_loc = lax.broadcasted_iota(jnp.int32, (2 * tq, tk), 1)

    for h in range(H_B):
        pair = qb_ref[:, h * LANES:(h + 1) * LANES]
        qs_ref[h, 0:tq, :] = jnp.where(lo_half, pair, jnp.zeros_like(pair))
        qs_ref[h, tq:2 * tq, :] = jnp.where(lo_half, jnp.zeros_like(pair), pair)
    m_ref[...] = jnp.full(m_ref.shape, NEG_INIT, F32)
    l_ref[...] = jnp.zeros(l_ref.shape, F32)
    acc_ref[...] = jnp.zeros(acc_ref.shape, F32)

    def step(kb, diag):
        k0 = pl.multiple_of(kb * tk, tk)
        for h in range(H_B):
            cols = slice(h * LANES, (h + 1) * LANES)
            s = _nt_dot(qs_ref[h], kb_ref[pl.ds(k0, tk), cols])
            if diag is not None:
                s = jnp.where(diag * tk + s_loc <= t_loc, s, NEG_MASKED)
            m_prev = m_ref[h]
            m_new = jnp.maximum(m_prev, jnp.max(s, axis=1, keepdims=True))
            alpha = jnp.exp2(m_prev - m_new)
            p = jnp.exp2(s - jnp.tile(m_new, (1, tk // LANES)))
            l_ref[h] = alpha * l_ref[h] + jnp.sum(p, axis=1, keepdims=True)
            acc_ref[h] = alpha * acc_ref[h] + _dot(p.astype(BF16), vb_ref[pl.ds(k0, tk), cols])
            m_ref[h] = m_new

    def body(kb, carry):
        step(kb, None)
        return carry

    lax.fori_loop(0, j * diag_blocks, body, 0)
    for dblk in range(diag_blocks):
        step(j * diag_blocks + dblk, dblk)

    for h in range(H_B):
        o = acc_ref[h] / l_ref[h]
        o = o[0:tq] - lam * o[tq:2 * tq]
        y = _rms_rows(o, gsub_ref[...]) * (1.0 - lam_init)
        o_ref[:, h * LANES:(h + 1) * LANES] = y.astype(BF16)


def _diffattn(qb, kb, vb, lam_p, gsub, tq, tk, lam_init):
    b, l, w = qb.shape
    nq = l // tq
    kern = functools.partial(_diff_kernel, tq=tq, tk=tk, lam_init=lam_init)
    per_q = pl.BlockSpec((None, tq, w), lambda bi, j: (bi, j, 0))
    per_b = pl.BlockSpec((None, l, w), lambda bi, j: (bi, 0, 0))
    full = lambda a: pl.BlockSpec(a.shape, lambda bi, j: (0,) * a.ndim)
    return pl.pallas_call(
        kern, grid=(b, nq),
        in_specs=[per_q, per_b, per_b, full(lam_p), full(gsub)],
        out_specs=per_q,
        out_shape=jax.ShapeDtypeStruct((b, l, w), BF16),
        scratch_shapes=[pltpu.VMEM((H_B, 2 * tq, LANES), BF16),
                        pltpu.VMEM((H_B, 2 * tq, LANES), F32),
                        pltpu.VMEM((H_B, 2 * tq, LANES), F32),
                        pltpu.VMEM((H_B, 2 * tq, LANES), F32)],
        compiler_params=_cparams(2), name="diffattn")(qb, kb, vb, lam_p, gsub)


def _merge_kernel(h_ref, g_ref, oa_ref, ob_ref, wg_ref, wa_ref, wb_ref, wo_ref, out_ref):
    h = h_ref[...]
    d = h.shape[-1]
    u = _rms_rows(h, g_ref[...]).astype(BF16)
    gates = jax.nn.sigmoid(_dot(u, wg_ref[...]))
    mix = (gates[:, 0:d] * _dot(oa_ref[...], wa_ref[...])
           + gates[:, d:2 * d] * _dot(ob_ref[...], wb_ref[...]))
    out_ref[...] = h + _dot(mix.astype(BF16), wo_ref[...])


def _layer_block(a, layer, **kw):
    return pl.BlockSpec((None,) + a.shape[1:], lambda i: (layer,) + (0,) * (a.ndim - 1), **kw)


def _merge(h, g, oa, ob, wg, wa, wb, wo, layer, tm):
    t, d = h.shape
    rows = lambda w: pl.BlockSpec((tm, w), lambda i: (i, 0))
    full = lambda a: pl.BlockSpec(a.shape, lambda i: (0,) * a.ndim)
    of_layer = lambda a: _layer_block(a, layer)
    return pl.pallas_call(
        _merge_kernel, grid=(t // tm,),
        in_specs=[rows(d), full(g), rows(oa.shape[1]), rows(ob.shape[1]), full(wg),
                  of_layer(wa), of_layer(wb), of_layer(wo)],
        out_specs=rows(d), out_shape=jax.ShapeDtypeStruct((t, d), F32),
        compiler_params=_cparams(1), name="merge")(h, g, oa, ob, wg, wa, wb, wo)


def _gelu_tanh(x):
    return 0.5 * x * (1.0 + jnp.tanh(math.sqrt(2.0 / math.pi) * (x + 0.044715 * (x * x * x))))


def _ffn_ple_kernel(h_ref, hn_ref, g_ref, wu_ref, cp_ref, wd_ref, p_ref, gp_ref, wpg_ref, wpp_ref, out_ref,
                    tail_ref, xa_ref, xb_ref, xn_ref, acc_ref, *, tiles_per_seq, cw):
    i = pl.program_id(0)
    tm = h_ref.shape[0]
    d_ff = wd_ref.shape[0]
    nc = d_ff // cw
    npan = cw // LANES
    h = h_ref[...]
    acc_ref[...] = jnp.zeros(acc_ref.shape, F32)

    def up_proj(u, c, x_ref, seq_start):
        for half, base in enumerate((0, d_ff)):
            cols = pl.ds(pl.multiple_of(base + c * cw, LANES), cw)
            x = _dot(u, wu_ref[:, cols])
            prev = jnp.where(seq_start, 0.0, tail_ref[:, cols])
            tail_ref[:, cols] = x[tm - SUBLANES:tm, :]
            for p in range(npan):
                x_ref[half * npan + p, 0:SUBLANES, :] = prev[:, p * LANES:(p + 1) * LANES]
                x_ref[half * npan + p, SUBLANES:SUBLANES + tm, :] = x[:, p * LANES:(p + 1) * LANES]

    def gated(c, x_ref):
        outs = []
        for p in range(npan):
            conv = []
            for half, base in enumerate((0, d_ff)):
                taps = cp_ref[:, pl.ds(pl.multiple_of(base + c * cw + p * LANES, LANES), LANES)]
                q = half * npan + p
                conv.append(taps[3:4, :]
                            + x_ref[q, SUBLANES - 2:SUBLANES - 2 + tm, :] * taps[0:1, :]
                            + x_ref[q, SUBLANES - 1:SUBLANES - 1 + tm, :] * taps[1:2, :]
                            + x_ref[q, SUBLANES:SUBLANES + tm, :] * taps[2:3, :])
            outs.append((_gelu_tanh(conv[0]) * conv[1]).astype(BF16))
        return jnp.concatenate(outs, axis=1)

    def down(c, act):
        acc_ref[...] += _dot(act, wd_ref[pl.ds(pl.multiple_of(c * cw, cw), cw), :])

    @pl.when(i == 0)
    def _():
        up_proj(_rms_rows(h, g_ref[...]).astype(BF16), 0, xn_ref, True)

    u = _rms_rows(h, g_ref[...]).astype(BF16)
    seq_start = (i % tiles_per_seq) == 0
    up_proj(u, 1, xa_ref, seq_start)
    down(0, gated(0, xn_ref))
    up_proj(u, 2, xb_ref, seq_start)
    down(1, gated(1, xa_ref))

    pairs = (nc - 3) // 2
    pairs_per_trip = 2 if pairs % 2 == 0 else 1

    def trip(t, carry):
        for r in range(pairs_per_trip):
            c = 2 * (t * pairs_per_trip + r) + 2
            up_proj(u, c + 1, xa_ref, seq_start)
            down(c, gated(c, xb_ref))
            up_proj(u, c + 2, xb_ref, seq_start)
            down(c + 1, gated(c + 1, xa_ref))
        return carry

    lax.fori_loop(0, pairs // pairs_per_trip, trip, 0)
    up_proj(_rms_rows(hn_ref[...], g_ref[...]).astype(BF16), 0, xn_ref, ((i + 1) % tiles_per_seq) == 0)
    down(nc - 1, gated(nc - 1, xb_ref))
    h2 = h + acc_ref[...]
    gate = jax.nn.sigmoid(_dot(_rms_rows(h2, gp_ref[...]).astype(BF16), wpg_ref[...]))
    out_ref[...] = h2 + gate * _dot(p_ref[...].astype(BF16), wpp_ref[...])


def _convffn_ple(h, g, wu, cp, wd, p, gp, wpg, wpp, layer, seq_len, tm, cw):
    t, d = h.shape
    n = t // tm
    nc = wd.shape[1] // cw
    assert nc % 2 == 1 and nc >= 3, "the chunk pipeline is written for an odd chunk count"
    rows = lambda w: pl.BlockSpec((tm, w), lambda i: (i, 0))
    next_rows = pl.BlockSpec((tm, d), lambda i: (jnp.minimum(i + 1, n - 1), 0))
    const = lambda a: pl.BlockSpec(a.shape, lambda i: (0,) * a.ndim, pipeline_mode=pl.Buffered(1))
    of_layer = lambda a: _layer_block(a, layer, pipeline_mode=pl.Buffered(1))
    p_rows = pl.BlockSpec((None, tm, p.shape[2]), lambda i: (layer, i, 0))
    kern = functools.partial(_ffn_ple_kernel, tiles_per_seq=seq_len // tm, cw=cw)
    panels = pltpu.VMEM((2 * cw // LANES, SUBLANES + tm, LANES), F32)
    return pl.pallas_call(
        kern, grid=(n,),
        in_specs=[rows(d), next_rows, const(g), of_layer(wu), const(cp), of_layer(wd),
                  p_rows, const(gp), of_layer(wpg), of_layer(wpp)],
        out_specs=rows(d), out_shape=jax.ShapeDtypeStruct((t, d), F32),
        scratch_shapes=[pltpu.VMEM((SUBLANES, wu.shape[2]), F32), panels, panels, panels,
                        pltpu.VMEM((tm, d), F32)],
        compiler_params=_cparams(1), name="convffn_ple")(h, h, g, wu, cp, wd, p, gp, wpg, wpp)


def _rope_tables(length):
    inv = 1.0 / (ROPE_THETA ** (jnp.arange(0, D_A, 2, dtype=F32) / D_A))
    ang = jnp.arange(length, dtype=F32)[:, None] * inv[None, :]
    c, s = jnp.cos(ang), jnp.sin(ang)
    return jnp.tile(c, (1, 4)), jnp.tile(jnp.concatenate([-s, s], axis=1), (1, 2))


def _pick_tile(n, pref):
    tile = min(n, pref)
    assert n % tile == 0, (n, tile)
    return tile


def kernel(x, p, g_mix_norm, w_in, g_qa, g_ka, g_qb, g_kb, lam_q1, lam_k1, lam_q2, lam_k2, g_subln,
           w_branch_a, w_branch_b, w_out, g_ffn_norm, w_up, conv_w, conv_b, w_down, g_ple_norm,
           w_ple_gate, w_ple_proj):
    bsz, seq_len, d_model = x.shape
    depth = w_in.shape[0]
    d_ff = w_down.shape[1]
    t = bsz * seq_len
    tq = _pick_tile(seq_len, 512)
    tk = _pick_tile(tq, 256)
    tk_diff = _pick_tile(tq, 512)
    tm = _pick_tile(seq_len, 512)
    tm_proj = _pick_tile(seq_len, 1024)
    cw = 256
    assert d_ff % cw == 0

    cos_t, sin_t = _rope_tables(seq_len)
    gsum = jnp.kron(jnp.eye(4, dtype=F32), jnp.ones((64, 64), F32)).astype(BF16)
    tabs = {"cos": cos_t, "sin": sin_t, "gsum": gsum}

    sizes = [H_A * D_A, D_A, D_A, H_I * D_I, D_I, H_I, 2 * H_B * D_B, 2 * H_B * D_B, 2 * H_B * D_B, d_model, d_model]
    offs = [0]
    for sz in sizes:
        offs.append(offs[-1] + sz)
    assert offs[-1] == w_in.shape[2]

    bf = lambda a: a.astype(BF16)
    w_a, w_b, w_o, w_u, w_d = bf(w_branch_a), bf(w_branch_b), bf(w_out), bf(w_up), bf(w_down)
    w_pg, w_pp = bf(w_ple_gate), bf(w_ple_proj)
    p_tok = p.reshape(depth, t, p.shape[-1])
    w_in2d = w_in.reshape(depth * d_model, w_in.shape[2])

    h = x.reshape(t, d_model)
    for i in range(depth):
        lam_init = 0.8 - 0.6 * math.exp(-0.3 * i)
        segs = lax.optimization_barrier([w_in2d[i * d_model:(i + 1) * d_model, offs[k]:offs[k + 1]]
                                         for k in range(len(sizes))])
        seg = lambda k: segs[k]
        wts = {
            "qa": seg(0).astype(BF16),
            "sm": jnp.concatenate([seg(1), seg(2), seg(4),
                                   jnp.pad(seg(5), ((0, 0), (0, D_I - H_I)))], axis=1).astype(BF16),
            "qi": seg(3).astype(BF16),
            "qb": seg(6).astype(BF16), "kb": seg(7).astype(BF16), "vb": seg(8).astype(BF16),
            "gqa": jnp.tile(g_qa[i], H_A)[None, :], "gka": jnp.tile(g_ka[i], 2)[None, :],
            "gqb": jnp.tile(g_qb[i], 2 * H_B)[None, :], "gkb": jnp.tile(g_kb[i], 2 * H_B)[None, :],
        }
        qa, ka, va, qi, ki, wi_t, qb, kb, vb = _in_proj(h, g_mix_norm[i][None, :], wts, tabs, seq_len, tm_proj)
        r3 = lambda a: a.reshape(bsz, seq_len, a.shape[-1])
        o_a = _dsa(r3(qa), r3(qi), wi_t, r3(ki), r3(ka), r3(va), tq, tk)
        lam_p = jnp.stack([lam_q1[i], lam_k1[i], lam_q2[i], lam_k2[i]])
        o_b = _diffattn(r3(qb), r3(kb), r3(vb), lam_p, g_subln[i][None, :], tq, tk_diff, lam_init)

        w_gate = jnp.concatenate([seg(9), seg(10)], axis=1).astype(BF16)
        h = _merge(h, g_mix_norm[i][None, :], o_a.reshape(t, -1), o_b.reshape(t, -1), w_gate,
                   w_a, w_b, w_o, i, tm_proj)

        conv = jnp.concatenate([conv_w[i], conv_b[i][None, :]], axis=0)
        conv = jnp.pad(conv, ((0, SUBLANES - CONV_W - 1), (0, 0)))
        h = _convffn_ple(h, g_ffn_norm[i][None, :], w_u, conv, w_d, p_tok, g_ple_norm[i][None, :], w_pg, w_pp,
                         i, seq_len, tm, cw)
    return h.reshape(bsz, seq_len, d_model)
```

```python
import functools
import math

import jax
import jax.numpy as jnp
from jax import lax
from jax.experimental import pallas as pl
from jax.experimental.pallas import tpu as pltpu

H_A, D_A = 8, 64
H_I, D_I = 4, 64
H_B, D_B = 4, 64
TOPK_MAX = 256
CONV_W = 3
ROPE_THETA = 10000.0
EPS = 1e-6

LANES = 128
SUBLANES = 8
PACKED_ROWS = 16
VMEM_LIMIT = 56 * 1024 * 1024

F32 = jnp.float32
BF16 = jnp.bfloat16
NEG_MASKED = -2e30
NEG_INIT = -1e30
INT_MIN = -(2 ** 31)
INT_MAX = 2 ** 31 - 1
LOG2E = math.log2(math.e)


def _nt_dot(a, b):
    return lax.dot_general(a, b, (((1,), (1,)), ((), ())), preferred_element_type=F32)


def _dot(a, b):
    return jnp.dot(a, b, preferred_element_type=F32)


def _cparams(n_axes):
    return pltpu.CompilerParams(dimension_semantics=("arbitrary",) * n_axes,
                                vmem_limit_bytes=VMEM_LIMIT)


def _rms_rows(x, g):
    ms = jnp.mean(x * x, axis=-1, keepdims=True)
    return x * lax.rsqrt(ms + EPS) * g


def _rope(x, cos_t, sin_t):
    w = x.shape[-1]
    reps = w // LANES
    c = jnp.tile(cos_t, (1, reps)) if reps > 1 else cos_t
    s = jnp.tile(sin_t, (1, reps)) if reps > 1 else sin_t
    up = pltpu.roll(x, w - 32, 1)
    dn = pltpu.roll(x, 32, 1)
    lane = lax.broadcasted_iota(jnp.int32, x.shape, 1)
    partner = jnp.where((lane & 63) < 32, up, dn)
    return x * c + partner * s


def _head_norm(x, gsum, g):
    slab = gsum.shape[0]
    sq = (x * x).astype(BF16)
    parts = [_dot(sq[:, c:c + slab], gsum) for c in range(0, x.shape[-1], slab)]
    ss = (parts[0] if len(parts) == 1 else jnp.concatenate(parts, axis=1)) * (1.0 / 64.0)
    return x * lax.rsqrt(ss + EPS) * g


def _in_proj_kernel(h_ref, g_ref, wqa_ref, wsm_ref, wqi_ref, wqb_ref, wkb_ref, wvb_ref,
                    cos_ref, sin_ref, gsum_ref, gqa_ref, gka_ref, gqb_ref, gkb_ref,
                    qa_ref, ka_ref, va_ref, qi_ref, ki_ref, wi_ref, qb_ref, kb_ref, vb_ref):
    u = _rms_rows(h_ref[...], g_ref[...]).astype(BF16)
    cos_t, sin_t = cos_ref[...], sin_ref[...]
    gsum = gsum_ref[...]

    qa = _rope(_head_norm(_dot(u, wqa_ref[...]), gsum, gqa_ref[...]), cos_t, sin_t)
    qa_ref[...] = (qa * (D_A ** -0.5 * LOG2E)).astype(BF16)

    sm = _dot(u, wsm_ref[...])
    kv, kw = sm[:, 0:LANES], sm[:, LANES:2 * LANES]
    kv_swapped, kw_swapped = pltpu.roll(kv, D_A, 1), pltpu.roll(kw, D_I, 1)
    lo_half = lax.broadcasted_iota(jnp.int32, (sm.shape[0], LANES), 1) < D_A
    ka2 = jnp.where(lo_half, kv, kv_swapped)
    ka_ref[...] = _rope(_head_norm(ka2, gsum[0:LANES, 0:LANES], gka_ref[...]), cos_t, sin_t).astype(BF16)
    va_ref[...] = jnp.where(lo_half, kv_swapped, 1.0).astype(BF16)
    ki_ref[...] = _rope(jnp.where(lo_half, kw, kw_swapped), cos_t, sin_t).astype(BF16)

    qi_ref[...] = _rope(_dot(u, wqi_ref[...]), cos_t, sin_t).astype(BF16)
    wi_t = kw.T
    wi_ref[...] = wi_t[D_I:D_I + SUBLANES, :] * (H_I ** -0.5 * D_I ** -0.5)

    qb = _rope(_head_norm(_dot(u, wqb_ref[...]), gsum, gqb_ref[...]), cos_t, sin_t)
    qb_ref[...] = (qb * (D_B ** -0.5 * LOG2E)).astype(BF16)
    kb = _rope(_head_norm(_dot(u, wkb_ref[...]), gsum, gkb_ref[...]), cos_t, sin_t)
    kb_ref[...] = kb.astype(BF16)
    vb_ref[...] = _dot(u, wvb_ref[...]).astype(BF16)


def _in_proj(h, g, w, tabs, seq_len, tm):
    t, d = h.shape
    nseq = seq_len // tm

    def full(a):
        return pl.BlockSpec(a.shape, lambda i: (0,) * a.ndim)

    def rows(width):
        return pl.BlockSpec((tm, width), lambda i: (i, 0))

    tab = pl.BlockSpec((tm, LANES), lambda i: (i % nseq, 0))
    ins = [h, g, w["qa"], w["sm"], w["qi"], w["qb"], w["kb"], w["vb"],
           tabs["cos"], tabs["sin"], tabs["gsum"], w["gqa"], w["gka"], w["gqb"], w["gkb"]]
    in_specs = [rows(d), full(g)] + [full(a) for a in ins[2:8]] + [tab, tab] + [full(a) for a in ins[10:]]
    widths = [H_A * D_A, LANES, LANES, H_I * D_I, LANES, None, 2 * H_B * D_B, 2 * H_B * D_B, 2 * H_B * D_B]
    out_shape, out_specs = [], []
    for wd in widths:
        if wd is None:
            out_shape.append(jax.ShapeDtypeStruct((SUBLANES, t), F32))
            out_specs.append(pl.BlockSpec((SUBLANES, tm), lambda i: (0, i)))
        else:
            out_shape.append(jax.ShapeDtypeStruct((t, wd), BF16))
            out_specs.append(rows(wd))
    return pl.pallas_call(
        _in_proj_kernel, grid=(t // tm,), in_specs=in_specs, out_specs=out_specs, out_shape=out_shape,
        compiler_params=_cparams(1), name="in_proj")(*ins)


def _key_to_float(k):
    return lax.bitcast_convert_type(k ^ ((k >> 31) & 0x7FFFFFFF), F32)


def _dsa_kernel(qa_ref, qi_ref, wi_ref, ki_ref, ka_ref, va_ref, ltri_ref, o_ref,
                score_ref, coarse_ref, bias_ref, qs_ref, m_ref, acc_ref,
                *, tq, tk, k_top):
    j = pl.program_id(1)
    nkb = (j + 1) * (tq // tk)
    nh = H_A
    groups = tk // SUBLANES
    groups16 = tk // PACKED_ROWS
    no_limit = 2 ** 30
    per_trip = 2 if (tq // tk) % 2 == 0 else 1

    def for_blocks(block_fn, init):
        def trip(t, carry):
            for r in range(per_trip):
                carry = block_fn(t * per_trip + r, carry)
            return carry
        return lax.fori_loop(0, nkb // per_trip, trip, init)

    lane_q = lax.broadcasted_iota(jnp.int32, (tq, LANES), 1)
    lo_half = lane_q < 64

    qi = qi_ref[...]
    qi_heads = []
    for h in range(H_I):
        pair = qi[:, (h // 2) * LANES:(h // 2 + 1) * LANES]
        keep = lo_half if h % 2 == 0 else jnp.logical_not(lo_half)
        qi_heads.append(jnp.where(keep, pair, jnp.zeros_like(pair)))
    wi = wi_ref[...]
    t_idx = j * tq + lax.broadcasted_iota(jnp.int32, (tk, tq), 1)
    s_loc = lax.broadcasted_iota(jnp.int32, (tk, tq), 0)

    def score_body(kb, carry):
        kblk = ki_ref[pl.ds(pl.multiple_of(kb * tk, tk), tk), :]
        acc = jnp.zeros((tk, tq), F32)
        for h in range(H_I):
            acc = acc + wi[h:h + 1, :] * jnp.maximum(_nt_dot(kblk, qi_heads[h]), 0.0)
        sc = jnp.where(kb * tk + s_loc <= t_idx, acc, -jnp.inf)
        score_ref[kb] = sc.reshape(groups, SUBLANES, tq)
        coarse_ref[kb] = sc.astype(BF16).reshape(groups16, PACKED_ROWS, tq)
        return carry

    for_blocks(score_body, 0)

    ways = 4

    def count(pred):
        def body(kb, acc):
            hit = pred(score_ref[kb]).astype(jnp.int32)
            part = groups // ways
            sums = [jnp.sum(hit[w * part:(w + 1) * part], axis=0) for w in range(ways)]
            return acc + ((sums[0] + sums[1]) + (sums[2] + sums[3]))
        acc = for_blocks(body, jnp.zeros((SUBLANES, tq), jnp.int32))
        return jnp.sum(acc, axis=0, keepdims=True)

    def count_coarse(c16):
        one, zero = jnp.ones((), BF16), jnp.zeros((), BF16)

        def body(kb, acc):
            hit = jnp.where(coarse_ref[kb] >= c16, one, zero)
            part = groups16 // ways
            sums = [functools.reduce(lambda a, b: a + b, [hit[w * part + r] for r in range(part)])
                    for w in range(ways)]
            return acc + ((sums[0] + sums[1]) + (sums[2] + sums[3])).astype(F32)
        acc = for_blocks(body, jnp.zeros((PACKED_ROWS, tq), F32))
        return jnp.sum(acc, axis=0, keepdims=True)

    def bcast8(v):
        return jnp.broadcast_to(v, (SUBLANES, tq))[None]

    def coarse_body(step, thr16):
        cand = jnp.where(step == 0, 0, thr16 | jnp.left_shift(1, jnp.maximum(15 - step, 0)))
        bits = jnp.left_shift(cand ^ ((cand >> 15) & 0x7FFF), 16)
        c16 = lax.bitcast_convert_type(bits, F32).astype(BF16)
        n_ge = count_coarse(jnp.broadcast_to(c16, (PACKED_ROWS, tq))[None])
        return jnp.where(n_ge >= k_top, cand, thr16)

    thr16 = lax.fori_loop(0, 16, coarse_body, jnp.full((1, tq), -(2 ** 15), jnp.int32))
    base = jnp.left_shift(thr16, 16)
    lo_key = jnp.where(base < INT_MIN + 2 ** 16, INT_MIN, base - 2 ** 16)

    def fine_body(step, thr_key):
        cand = thr_key + jnp.left_shift(1, 17 - step)
        cand = jnp.where((thr_key > 0) & (cand < 0), INT_MAX, cand)
        c8 = bcast8(_key_to_float(cand))
        n_ge = count(lambda blk: blk >= c8)
        return jnp.where(n_ge >= k_top, cand, thr_key)

    thr_key = lax.fori_loop(0, 18, fine_body, lo_key)
    n_valid = j * tq + lax.broadcasted_iota(jnp.int32, (1, tq), 1) + 1
    take_all = n_valid <= k_top
    thr = jnp.where(take_all, -jnp.inf, _key_to_float(thr_key))
    t8 = bcast8(thr)
    n_gt = count(lambda blk: blk > t8)
    need = jnp.where(take_all, no_limit, k_top - n_gt).astype(F32)

    def bias_body(kb, ties_before):
        sc = score_ref[kb].reshape(tk, tq)
        eq = sc == thr
        incl = _dot(ltri_ref[...], jnp.where(eq, 1.0, 0.0).astype(BF16))
        sel = (sc > thr) | (eq & (incl + ties_before <= need))
        sel = sel & (kb * tk + s_loc <= t_idx)
        bias_ref[kb] = jnp.where(sel, 0.0, NEG_MASKED).astype(F32).T
        return ties_before + incl[tk - 1:tk, :]

    for_blocks(bias_body, jnp.zeros((1, tq), F32))

    qa = qa_ref[...]
    for h in range(nh):
        pair = qa[:, (h // 2) * LANES:(h // 2 + 1) * LANES]
        keep = lo_half if h % 2 == 0 else jnp.logical_not(lo_half)
        qs_ref[h * tq:(h + 1) * tq, :] = jnp.where(keep, pair, jnp.zeros_like(pair))
    m_ref[...] = jnp.full(m_ref.shape, NEG_INIT, F32)
    acc_ref[...] = jnp.zeros(acc_ref.shape, F32)

    def attn_body(kb, carry):
        k0 = pl.multiple_of(kb * tk, tk)
        kblk = ka_ref[pl.ds(k0, tk), :]
        vblk = va_ref[pl.ds(k0, tk), :]
        for h in range(nh):
            rows = slice(h * tq, (h + 1) * tq)
            s = _nt_dot(qs_ref[rows, :], kblk) + bias_ref[kb]
            m_prev = m_ref[rows, :]
            m_new = jnp.maximum(m_prev, jnp.max(s, axis=1, keepdims=True))
            p = jnp.exp2(s - jnp.tile(m_new, (1, tk // LANES)))
            acc_ref[rows, :] = acc_ref[rows, :] * jnp.exp2(m_prev - m_new) + _dot(p.astype(BF16), vblk)
            m_ref[rows, :] = m_new
        return carry

    for_blocks(attn_body, 0)

    for pr in range(nh // 2):
        even = acc_ref[2 * pr * tq:(2 * pr + 1) * tq, :]
        odd = acc_ref[(2 * pr + 1) * tq:(2 * pr + 2) * tq, :]
        num = jnp.where(lo_half, even, pltpu.roll(odd, 64, 1))
        den = jnp.where(lo_half, pltpu.roll(even, 64, 1), odd)
        o_ref[:, pr * LANES:(pr + 1) * LANES] = (num / den).astype(BF16)


def _dsa(qa, qi, wi_t, ki, ka, va, tq, tk):
    b, l, _ = qa.shape
    nq = l // tq
    k_top = min(TOPK_MAX, l // 4)
    kern = functools.partial(_dsa_kernel, tq=tq, tk=tk, k_top=k_top)
    per_q = lambda w: pl.BlockSpec((None, tq, w), lambda bi, j: (bi, j, 0))
    per_b = pl.BlockSpec((None, l, LANES), lambda bi, j: (bi, 0, 0))
    ltri = jnp.tril(jnp.ones((tk, tk), F32)).astype(BF16)
    return pl.pallas_call(
        kern, grid=(b, nq),
        in_specs=[per_q(H_A * D_A), per_q(H_I * D_I),
                  pl.BlockSpec((SUBLANES, tq), lambda bi, j: (0, bi * nq + j)),
                  per_b, per_b, per_b,
                  pl.BlockSpec((tk, tk), lambda bi, j: (0, 0))],
        out_specs=per_q(H_A * D_A),
        out_shape=jax.ShapeDtypeStruct((b, l, H_A * D_A), BF16),
        scratch_shapes=[
            pltpu.VMEM((l // tk, tk // SUBLANES, SUBLANES, tq), F32),
            pltpu.VMEM((l // tk, tk // PACKED_ROWS, PACKED_ROWS, tq), BF16),
            pltpu.VMEM((l // tk, tq, tk), F32),
            pltpu.VMEM((H_A * tq, LANES), BF16),
            pltpu.VMEM((H_A * tq, LANES), F32),
            pltpu.VMEM((H_A * tq, LANES), F32),
        ],
        compiler_params=_cparams(2), name="dsa")(qa, qi, wi_t, ki, ka, va, ltri)


def _diff_kernel(qb_ref, kb_ref, vb_ref, lam_ref, gsub_ref, o_ref, qs_ref, m_ref, l_ref, acc_ref,
                 *, tq, tk, lam_init):
    j = pl.program_id(1)
    diag_blocks = tq // tk
    lamv = lam_ref[...]
    lam = (jnp.exp(jnp.sum(lamv[0:1] * lamv[1:2], axis=1, keepdims=True))
           - jnp.exp(jnp.sum(lamv[2:3] * lamv[3:4], axis=1, keepdims=True)) + lam_init)
    lane_q = lax.broadcasted_iota(jnp.int32, (tq, LANES), 1)
    lo_half = lane_q < 64
    row = lax.broadcasted_iota(jnp.int32, (2 * tq, tk), 0)
    t_loc = jnp.where(row >= tq, row - tq, row)
    s_loc = lax.broadcasted_iota(jnp.int32, (2 * tq, tk), 1)

    for h in range(H_B):
        pair = qb_ref[:, h * LANES:(h + 1) * LANES]
        qs_ref[h, 0:tq, :] = jnp.where(lo_half, pair, jnp.zeros_like(pair))
        qs_ref[h, tq:2 * tq, :] = jnp.where(lo_half, jnp.zeros_like(pair), pair)
    def step(kb, diag, first=False):
        k0 = pl.multiple_of(kb * tk, tk)
        for h in range(H_B):
            cols = slice(h * LANES, (h + 1) * LANES)
            s = _nt_dot(qs_ref[h], kb_ref[pl.ds(k0, tk), cols])
            if diag is not None:
                s = jnp.where(diag * tk + s_loc <= t_loc, s, NEG_MASKED)
            m_cur = jnp.max(s, axis=1, keepdims=True)
            m_new = jnp.broadcast_to(m_cur, (2 * tq, LANES)) if first else jnp.maximum(m_ref[h], m_cur)
            p = jnp.exp2(s - jnp.tile(m_new, (1, tk // LANES)))
            p_sum = jnp.sum(p, axis=1, keepdims=True)
            pv = _dot(p.astype(BF16), vb_ref[pl.ds(k0, tk), cols])
            if first:
                l_ref[h] = jnp.broadcast_to(p_sum, (2 * tq, LANES))
                acc_ref[h] = pv
            else:
                alpha = jnp.exp2(m_ref[h] - m_new)
                l_ref[h] = alpha * l_ref[h] + p_sum
                acc_ref[h] = alpha * acc_ref[h] + pv
            m_ref[h] = m_new

    for dblk in range(diag_blocks):
        step(j * diag_blocks + dblk, dblk, first=(dblk == 0))

    def body(kb, carry):
        step(kb, None)
        return carry

    lax.fori_loop(0, j * diag_blocks, body, 0)

    for h in range(H_B):
        o = acc_ref[h] / l_ref[h]
        o = o[0:tq] - lam * o[tq:2 * tq]
        y = _rms_rows(o, gsub_ref[...]) * (1.0 - lam_init)
        o_ref[:, h * LANES:(h + 1) * LANES] = y.astype(BF16)


def _diffattn(qb, kb, vb, lam_p, gsub, tq, tk, lam_init):
    b, l, w = qb.shape
    nq = l // tq
    kern = functools.partial(_diff_kernel, tq=tq, tk=tk, lam_init=lam_init)
    per_q = pl.BlockSpec((None, tq, w), lambda bi, j: (bi, j, 0))
    per_b = pl.BlockSpec((None, l, w), lambda bi, j: (bi, 0, 0))
    full = lambda a: pl.BlockSpec(a.shape, lambda bi, j: (0,) * a.ndim)
    return pl.pallas_call(
        kern, grid=(b, nq),
        in_specs=[per_q, per_b, per_b, full(lam_p), full(gsub)],
        out_specs=per_q,
        out_shape=jax.ShapeDtypeStruct((b, l, w), BF16),
        scratch_shapes=[pltpu.VMEM((H_B, 2 * tq, LANES), BF16),
                        pltpu.VMEM((H_B, 2 * tq, LANES), F32),
                        pltpu.VMEM((H_B, 2 * tq, LANES), F32),
                        pltpu.VMEM((H_B, 2 * tq, LANES), F32)],
        compiler_params=_cparams(2), name="diffattn")(qb, kb, vb, lam_p, gsub)


def _merge_kernel(h_ref, g_ref, oa_ref, ob_ref, wg_ref, wa_ref, wb_ref, wo_ref, out_ref):
    h = h_ref[...]
    d = h.shape[-1]
    u = _rms_rows(h, g_ref[...]).astype(BF16)
    gates = jax.nn.sigmoid(_dot(u, wg_ref[...]))
    mix = (gates[:, 0:d] * _dot(oa_ref[...], wa_ref[...])
           + gates[:, d:2 * d] * _dot(ob_ref[...], wb_ref[...]))
    out_ref[...] = h + _dot(mix.astype(BF16), wo_ref[...])


def _layer_block(a, layer, **kw):
    return pl.BlockSpec((None,) + a.shape[1:], lambda i: (layer,) + (0,) * (a.ndim - 1), **kw)


def _merge(h, g, oa, ob, wg, wa, wb, wo, layer, tm):
    t, d = h.shape
    rows = lambda w: pl.BlockSpec((tm, w), lambda i: (i, 0))
    full = lambda a: pl.BlockSpec(a.shape, lambda i: (0,) * a.ndim)
    of_layer = lambda a: _layer_block(a, layer)
    return pl.pallas_call(
        _merge_kernel, grid=(t // tm,),
        in_specs=[rows(d), full(g), rows(oa.shape[1]), rows(ob.shape[1]), full(wg),
                  of_layer(wa), of_layer(wb), of_layer(wo)],
        out_specs=rows(d), out_shape=jax.ShapeDtypeStruct((t, d), F32),
        compiler_params=_cparams(1), name="merge")(h, g, oa, ob, wg, wa, wb, wo)


def _gelu_tanh(x):
    return 0.5 * x * (1.0 + jnp.tanh(math.sqrt(2.0 / math.pi) * (x + 0.044715 * (x * x * x))))


def _ffn_ple_kernel(h_ref, hn_ref, g_ref, wu_ref, cp_ref, wd_ref, p_ref, gp_ref, wpg_ref, wpp_ref, out_ref,
                    tail_ref, xa_ref, xb_ref, xn_ref, acc_ref, *, tiles_per_seq, cw):
    i = pl.program_id(0)
    tm = h_ref.shape[0]
    d_ff = wd_ref.shape[0]
    nc = d_ff // cw
    npan = cw // LANES
    h = h_ref[...]

    def up_proj(u, c, x_ref, seq_start):
        for half, base in enumerate((0, d_ff)):
            cols = pl.ds(pl.multiple_of(base + c * cw, LANES), cw)
            x = _dot(u, wu_ref[:, cols])
            prev = jnp.where(seq_start, 0.0, tail_ref[:, cols])
            tail_ref[:, cols] = x[tm - SUBLANES:tm, :]
            for p in range(npan):
                x_ref[half * npan + p, 0:SUBLANES, :] = prev[:, p * LANES:(p + 1) * LANES]
                x_ref[half * npan + p, SUBLANES:SUBLANES + tm, :] = x[:, p * LANES:(p + 1) * LANES]

    def gated(c, x_ref):
        outs = []
        for p in range(npan):
            conv = []
            for half, base in enumerate((0, d_ff)):
                taps = cp_ref[:, pl.ds(pl.multiple_of(base + c * cw + p * LANES, LANES), LANES)]
                q = half * npan + p
                conv.append(taps[3:4, :]
                            + x_ref[q, SUBLANES - 2:SUBLANES - 2 + tm, :] * taps[0:1, :]
                            + x_ref[q, SUBLANES - 1:SUBLANES - 1 + tm, :] * taps[1:2, :]
                            + x_ref[q, SUBLANES:SUBLANES + tm, :] * taps[2:3, :])
            outs.append((_gelu_tanh(conv[0]) * conv[1]).astype(BF16))
        return jnp.concatenate(outs, axis=1)

    def down(c, act, first=False):
        part = _dot(act, wd_ref[pl.ds(pl.multiple_of(c * cw, cw), cw), :])
        acc_ref[...] = part if first else acc_ref[...] + part

    @pl.when(i == 0)
    def _():
        up_proj(_rms_rows(h, g_ref[...]).astype(BF16), 0, xn_ref, True)

    u = _rms_rows(h, g_ref[...]).astype(BF16)
    seq_start = (i % tiles_per_seq) == 0
    up_proj(u, 1, xa_ref, seq_start)
    down(0, gated(0, xn_ref), first=True)
    up_proj(u, 2, xb_ref, seq_start)
    down(1, gated(1, xa_ref))

    pairs = (nc - 3) // 2
    pairs_per_trip = 2 if pairs % 2 == 0 else 1

    def trip(t, carry):
        for r in range(pairs_per_trip):
            c = 2 * (t * pairs_per_trip + r) + 2
            up_proj(u, c + 1, xa_ref, seq_start)
            down(c, gated(c, xb_ref))
            up_proj(u, c + 2, xb_ref, seq_start)
            down(c + 1, gated(c + 1, xa_ref))
        return carry

    lax.fori_loop(0, pairs // pairs_per_trip, trip, 0)
    up_proj(_rms_rows(hn_ref[...], g_ref[...]).astype(BF16), 0, xn_ref, ((i + 1) % tiles_per_seq) == 0)
    down(nc - 1, gated(nc - 1, xb_ref))
    h2 = h + acc_ref[...]
    gate = jax.nn.sigmoid(_dot(_rms_rows(h2, gp_ref[...]).astype(BF16), wpg_ref[...]))
    out_ref[...] = h2 + gate * _dot(p_ref[...].astype(BF16), wpp_ref[...])


def _convffn_ple(h, g, wu, cp, wd, p, gp, wpg, wpp, layer, seq_len, tm, cw):
    t, d = h.shape
    n = t // tm
    nc = wd.shape[1] // cw
    assert nc % 2 == 1 and nc >= 3, "the chunk pipeline is written for an odd chunk count"
    rows = lambda w: pl.BlockSpec((tm, w), lambda i: (i, 0))
    next_rows = pl.BlockSpec((tm, d), lambda i: (jnp.minimum(i + 1, n - 1), 0))
    const = lambda a: pl.BlockSpec(a.shape, lambda i: (0,) * a.ndim, pipeline_mode=pl.Buffered(1))
    of_layer = lambda a: _layer_block(a, layer, pipeline_mode=pl.Buffered(1))
    p_rows = pl.BlockSpec((None, tm, p.shape[2]), lambda i: (layer, i, 0))
    kern = functools.partial(_ffn_ple_kernel, tiles_per_seq=seq_len // tm, cw=cw)
    panels = pltpu.VMEM((2 * cw // LANES, SUBLANES + tm, LANES), F32)
    return pl.pallas_call(
        kern, grid=(n,),
        in_specs=[rows(d), next_rows, const(g), of_layer(wu), const(cp), of_layer(wd),
                  p_rows, const(gp), of_layer(wpg), of_layer(wpp)],
        out_specs=rows(d), out_shape=jax.ShapeDtypeStruct((t, d), F32),
        scratch_shapes=[pltpu.VMEM((SUBLANES, wu.shape[2]), F32), panels, panels, panels,
                        pltpu.VMEM((tm, d), F32)],
        compiler_params=_cparams(1), name="convffn_ple")(h, h, g, wu, cp, wd, p, gp, wpg, wpp)


def _rope_tables(length):
    inv = 1.0 / (ROPE_THETA ** (jnp.arange(0, D_A, 2, dtype=F32) / D_A))
    ang = jnp.arange(length, dtype=F32)[:, None] * inv[None, :]
    c, s = jnp.cos(ang), jnp.sin(ang)
    return jnp.tile(c, (1, 4)), jnp.tile(jnp.concatenate([-s, s], axis=1), (1, 2))


def _pick_tile(n, pref):
    tile = min(n, pref)
    assert n % tile == 0, (n, tile)
    return tile


def kernel(x, p, g_mix_norm, w_in, g_qa, g_ka, g_qb, g_kb, lam_q1, lam_k1, lam_q2, lam_k2, g_subln,
           w_branch_a, w_branch_b, w_out, g_ffn_norm, w_up, conv_w, conv_b, w_down, g_ple_norm,
           w_ple_gate, w_ple_proj):
    bsz, seq_len, d_model = x.shape
    depth = w_in.shape[0]
    d_ff = w_down.shape[1]
    t = bsz * seq_len
    tq = _pick_tile(seq_len, 512)
    tk = _pick_tile(tq, 256)
    tk_diff = _pick_tile(tq, 512)
    tm = _pick_tile(seq_len, 512)
    tm_proj = _pick_tile(seq_len, 1024)
    cw = 256
    assert d_ff % cw == 0

    cos_t, sin_t = _rope_tables(seq_len)
    gsum = jnp.kron(jnp.eye(4, dtype=F32), jnp.ones((64, 64), F32)).astype(BF16)
    tabs = {"cos": cos_t, "sin": sin_t, "gsum": gsum}

    sizes = [H_A * D_A, D_A, D_A, H_I * D_I, D_I, H_I, 2 * H_B * D_B, 2 * H_B * D_B, 2 * H_B * D_B, d_model, d_model]
    offs = [0]
    for sz in sizes:
        offs.append(offs[-1] + sz)
    assert offs[-1] == w_in.shape[2]

    bf = lambda a: a.astype(BF16)
    w_a, w_b, w_o, w_u, w_d = bf(w_branch_a), bf(w_branch_b), bf(w_out), bf(w_up), bf(w_down)
    w_pg, w_pp = bf(w_ple_gate), bf(w_ple_proj)
    p_tok = p.reshape(depth, t, p.shape[-1])
    w_in2d = w_in.reshape(depth * d_model, w_in.shape[2])

    h = x.reshape(t, d_model)
    for i in range(depth):
        lam_init = 0.8 - 0.6 * math.exp(-0.3 * i)
        segs = lax.optimization_barrier([w_in2d[i * d_model:(i + 1) * d_model, offs[k]:offs[k + 1]]
                                         for k in range(len(sizes))])
        seg = lambda k: segs[k]
        wts = {
            "qa": seg(0).astype(BF16),
            "sm": jnp.concatenate([seg(1), seg(2), seg(4),
                                   jnp.pad(seg(5), ((0, 0), (0, D_I - H_I)))], axis=1).astype(BF16),
            "qi": seg(3).astype(BF16),
            "qb": seg(6).astype(BF16), "kb": seg(7).astype(BF16), "vb": seg(8).astype(BF16),
            "gqa": jnp.tile(g_qa[i], H_A)[None, :], "gka": jnp.tile(g_ka[i], 2)[None, :],
            "gqb": jnp.tile(g_qb[i], 2 * H_B)[None, :], "gkb": jnp.tile(g_kb[i], 2 * H_B)[None, :],
        }
        qa, ka, va, qi, ki, wi_t, qb, kb, vb = _in_proj(h, g_mix_norm[i][None, :], wts, tabs, seq_len, tm_proj)
        r3 = lambda a: a.reshape(bsz, seq_len, a.shape[-1])
        o_a = _dsa(r3(qa), r3(qi), wi_t, r3(ki), r3(ka), r3(va), tq, tk)
        lam_p = jnp.stack([lam_q1[i], lam_k1[i], lam_q2[i], lam_k2[i]])
        o_b = _diffattn(r3(qb), r3(kb), r3(vb), lam_p, g_subln[i][None, :], tq, tk_diff, lam_init)

        w_gate = jnp.concatenate([seg(9), seg(10)], axis=1).astype(BF16)
        h = _merge(h, g_mix_norm[i][None, :], o_a.reshape(t, -1), o_b.reshape(t, -1), w_gate,
                   w_a, w_b, w_o, i, tm_proj)

        conv = jnp.concatenate([conv_w[i], conv_b[i][None, :]], axis=0)
        conv = jnp.pad(conv, ((0, SUBLANES - CONV_W - 1), (0, 0)))
        h = _convffn_ple(h, g_ffn_norm[i][None, :], w_u, conv, w_d, p_tok, g_ple_norm[i][None, :], w_pg, w_pp,
                         i, seq_len, tm, cw)
    return h.reshape(bsz, seq_len, d_model)
```
